```python
import jax, jax.numpy as jnp
from jax import lax
import numpy as np

D_MODEL = 1024
BATCH = 4
SEQ = 4096
DEPTH = 2

CTX_LEN = 256
GRID_W = 64
EPS = 1e-6
MIX_WIDTH = 2 * D_MODEL
SSD_WIDTH = MIX_WIDTH // 2
SSD_HEAD_DIM = 64
SSD_HEADS = SSD_WIDTH // SSD_HEAD_DIM
SSD_GROUPS = 2
SSD_HPG = SSD_HEADS // SSD_GROUPS
SSD_STATE = 128
SSD_CHUNK = 128
SSD_CONV = 3
SSD_XBC = SSD_WIDTH + 2 * SSD_GROUPS * SSD_STATE
SSD_COLS = SSD_WIDTH + SSD_XBC + 2 * SSD_HEADS
SC_WIDTH = MIX_WIDTH // 4
SC_CONV = 3
SC_COLS = 4 * SC_WIDTH
ATTN_HEAD_DIM = 64
ATTN_WIDTH = MIX_WIDTH // 4
ATTN_HEADS = ATTN_WIDTH // ATTN_HEAD_DIM
ATTN_KV_HEADS = 2
ATTN_REP = ATTN_HEADS // ATTN_KV_HEADS
ATTN_WINDOW = 128
ATTN_BLOCK = 128
ROPE_BASE = 10000.0
ATTN_COLS = 2 * ATTN_WIDTH + 2 * ATTN_KV_HEADS * ATTN_HEAD_DIM
IN_COLS = SSD_COLS + SC_COLS + ATTN_COLS
NEG_INF = -1e30

kernel_name = "hybrid_ssd_shortconv_swa_dit_block"


def rmsnorm(t, w):
    t32 = t.astype(jnp.float32)
    t32 = t32 * lax.rsqrt(jnp.mean(t32 * t32, axis=-1, keepdims=True) + EPS)
    return (t32 * w.astype(jnp.float32)).astype(t.dtype)


def dwconv(u, w):
    k, ch = w.shape
    return lax.conv_general_dilated(
        u, w[:, None, :].astype(u.dtype), window_strides=(1,), padding=[(k // 2, k // 2)],
        dimension_numbers=("NWC", "WIO", "NWC"), feature_group_count=ch)


def segsum(a):
    t = a.shape[-1]
    cs = jnp.cumsum(a, axis=-1)
    diff = cs[..., :, None] - cs[..., None, :]
    mask = jnp.tril(jnp.ones((t, t), dtype=bool))
    return jnp.where(mask, diff, -jnp.inf)


def ssd_scan(xs, da, bm, cm, init, with_y):
    b, l, g, r, p = xs.shape
    n = bm.shape[-1]
    nc = l // SSD_CHUNK
    xs = xs.reshape(b, nc, SSD_CHUNK, g, r, p)
    bm = bm.reshape(b, nc, SSD_CHUNK, g, n)
    cm = cm.reshape(b, nc, SSD_CHUNK, g, n)
    da = da.reshape(b, nc, SSD_CHUNK, g, r).transpose(0, 3, 4, 1, 2)
    a_cum = jnp.cumsum(da, axis=-1)
    decay_states = jnp.exp(a_cum[..., -1:] - a_cum)
    states = jnp.einsum("bclgn,bgrcl,bclgrp->bcgrpn", bm, decay_states, xs)
    chunk_tot = jnp.pad(a_cum[..., -1], ((0, 0), (0, 0), (0, 0), (1, 0)))
    decay_chunk = jnp.exp(segsum(chunk_tot))
    states = jnp.concatenate([init[:, None], states], axis=1)
    new_states = jnp.einsum("bgrzc,bcgrpn->bzgrpn", decay_chunk, states)
    final = new_states[:, -1]
    if not with_y:
        return None, final
    states = new_states[:, :-1]
    lmat = jnp.exp(segsum(da))
    cb = jnp.einsum("bclgn,bcsgn->bgcls", cm, bm)
    y_diag = jnp.einsum("bgcls,bgrcls,bcsgrp->bclgrp", cb, lmat, xs)
    y_off = jnp.einsum("bclgn,bcgrpn,bgrcl->bclgrp", cm, states, jnp.exp(a_cum))
    return (y_diag + y_off).reshape(b, l, g, r, p), final


def _flip(t, direction):
    return t[:, ::-1] if direction == 1 else t


def ssd_branch(pc, pl, conv_w, conv_b, dt_bias, a_log, d_skip, norm_w, with_ctx):
    def prep(p):
        b, l, _ = p.shape
        z, xbc, dt = jnp.split(p, [SSD_WIDTH, SSD_WIDTH + SSD_XBC], axis=-1)
        xbc = jax.nn.silu(dwconv(xbc, conv_w) + conv_b.astype(xbc.dtype))
        xs, bm, cm = jnp.split(xbc, [SSD_WIDTH, SSD_WIDTH + SSD_GROUPS * SSD_STATE], axis=-1)
        return (z, xs.reshape(b, l, SSD_GROUPS, SSD_HPG, SSD_HEAD_DIM),
                bm.reshape(b, l, SSD_GROUPS, SSD_STATE), cm.reshape(b, l, SSD_GROUPS, SSD_STATE), dt)

    zc, xc, bc, cc, dtc = prep(pc)
    zl, xl, bl, cl, dtl = prep(pl)
    b = pl.shape[0]
    dsk = d_skip.astype(jnp.float32).reshape(SSD_GROUPS, SSD_HPG, 1)
    y_l = xl.astype(jnp.float32) * dsk
    y_c = xc.astype(jnp.float32) * dsk
    for d in range(2):
        a = -jnp.exp(a_log[d].astype(jnp.float32)).reshape(SSD_GROUPS, SSD_HPG)

        def disc(dt_raw):
            dt = jax.nn.softplus(dt_raw[..., d * SSD_HEADS:(d + 1) * SSD_HEADS].astype(jnp.float32)
                                 + dt_bias[d].astype(jnp.float32))
            return dt.reshape(dt.shape[0], dt.shape[1], SSD_GROUPS, SSD_HPG)

        dt_c, dt_l = disc(dtc), disc(dtl)
        init = jnp.zeros((b, SSD_GROUPS, SSD_HPG, SSD_HEAD_DIM, SSD_STATE), jnp.float32)
        yc_d, s_ctx = ssd_scan(_flip(xc * dt_c[..., None], d), _flip(dt_c * a, d),
                               _flip(bc, d), _flip(cc, d), init, with_ctx)
        yl_d, _ = ssd_scan(_flip(xl * dt_l[..., None], d), _flip(dt_l * a, d),
                           _flip(bl, d), _flip(cl, d), s_ctx, True)
        y_l = y_l + _flip(yl_d, d)
        if with_ctx:
            y_c = y_c + _flip(yc_d, d)

    def gated_norm(y, z):
        bb, l = y.shape[:2]
        g = y.reshape(bb, l, SSD_GROUPS, -1) * jax.nn.silu(z.astype(jnp.float32)).reshape(bb, l, SSD_GROUPS, -1)
        g = g * lax.rsqrt(jnp.mean(g * g, axis=-1, keepdims=True) + EPS)
        return (g.reshape(bb, l, SSD_WIDTH) * norm_w.astype(jnp.float32)).astype(z.dtype)

    out_c = gated_norm(y_c, zc) if with_ctx else None
    return out_c, gated_norm(y_l, zl)


def shortconv_branch(pc, pl, conv_w, with_ctx):
    def run(p):
        v, cg, bg, z = jnp.split(p, 4, axis=-1)
        return bg * dwconv(cg * v, conv_w) * jax.nn.silu(z)
    return (run(pc) if with_ctx else None), run(pl)


def rope_tables(n_lat):
    n_rows = n_lat // GRID_W
    rows = jnp.repeat(jnp.arange(n_rows, dtype=jnp.int32), GRID_W).astype(jnp.float32)
    cols = jnp.tile(jnp.arange(GRID_W, dtype=jnp.int32), n_rows).astype(jnp.float32)
    axis_dim = ATTN_HEAD_DIM // 2
    inv_freq = ROPE_BASE ** (-jnp.arange(0, axis_dim, 2, dtype=jnp.float32) / axis_dim)
    ang = jnp.concatenate([rows[:, None] * inv_freq, cols[:, None] * inv_freq], axis=-1)
    return jnp.cos(ang), jnp.sin(ang)


def apply_rope(t, cos, sin):
    b, l, h, d = t.shape
    q = d // 4
    t = t.reshape(b, l, h, 2, 2, q)
    x1, x2 = t[..., 0, :], t[..., 1, :]
    c = cos.reshape(l, 1, 2, q).astype(t.dtype)
    s = sin.reshape(l, 1, 2, q).astype(t.dtype)
    return jnp.stack([x1 * c - x2 * s, x1 * s + x2 * c], axis=-2).reshape(b, l, h, d)


def attn_branch(pc, pl, sink, cos, sin, with_ctx):
    scale = ATTN_HEAD_DIM ** -0.5

    def split(p):
        b, l, _ = p.shape
        q, k, v, z = jnp.split(p, [ATTN_WIDTH, ATTN_WIDTH + ATTN_KV_HEADS * ATTN_HEAD_DIM,
                                   ATTN_WIDTH + 2 * ATTN_KV_HEADS * ATTN_HEAD_DIM], axis=-1)
        return (q.reshape(b, l, ATTN_HEADS, ATTN_HEAD_DIM), k.reshape(b, l, ATTN_KV_HEADS, ATTN_HEAD_DIM),
                v.reshape(b, l, ATTN_KV_HEADS, ATTN_HEAD_DIM), z)

    qc, kc, vc, zc = split(pc)
    ql, kl, vl, zl = split(pl)
    ql, kl = apply_rope(ql, cos, sin), apply_rope(kl, cos, sin)
    b, n_lat = pl.shape[:2]
    nb = n_lat // ATTN_BLOCK
    sink32 = sink.astype(jnp.float32).reshape(ATTN_KV_HEADS, ATTN_REP)

    qb = ql.reshape(b, nb, ATTN_BLOCK, ATTN_KV_HEADS, ATTN_REP, ATTN_HEAD_DIM)

    def windows(t):
        tp = jnp.pad(t.reshape(b, nb, ATTN_BLOCK, ATTN_KV_HEADS, ATTN_HEAD_DIM),
                     ((0, 0), (1, 1), (0, 0), (0, 0), (0, 0)))
        return jnp.concatenate([tp[:, :-2], tp[:, 1:-1], tp[:, 2:]], axis=2)

    kw, vw = windows(kl), windows(vl)
    s_band = jnp.einsum("bnqkrd,bnskd->bnkrqs", qb, kw).astype(jnp.float32) * scale
    qpos = jnp.arange(nb)[:, None] * ATTN_BLOCK + jnp.arange(ATTN_BLOCK)[None, :]
    kpos = (jnp.arange(nb)[:, None] - 1) * ATTN_BLOCK + jnp.arange(3 * ATTN_BLOCK)[None, :]
    valid = ((jnp.abs(qpos[:, :, None] - kpos[:, None, :]) <= ATTN_WINDOW)
             & (kpos[:, None, :] >= 0) & (kpos[:, None, :] < n_lat))
    s_band = jnp.where(valid[None, :, None, None], s_band, NEG_INF)
    s_ctx = jnp.einsum("bnqkrd,bskd->bnkrqs", qb, kc).astype(jnp.float32) * scale
    s_sink = jnp.broadcast_to(sink32[None, None, :, :, None, None], s_band.shape[:-1] + (1,))
    probs = jax.nn.softmax(jnp.concatenate([s_band, s_ctx, s_sink], axis=-1), axis=-1)
    p_band = probs[..., :3 * ATTN_BLOCK].astype(vl.dtype)
    p_ctx = probs[..., 3 * ATTN_BLOCK:3 * ATTN_BLOCK + kc.shape[1]].astype(vl.dtype)
    o = (jnp.einsum("bnkrqs,bnskd->bnqkrd", p_band, vw)
         + jnp.einsum("bnkrqs,bskd->bnqkrd", p_ctx, vc))
    out_l = o.reshape(b, n_lat, ATTN_WIDTH) * jax.nn.silu(zl)

    out_c = None
    if with_ctx:
        n_ctx = pc.shape[1]
        qcg = qc.reshape(b, n_ctx, ATTN_KV_HEADS, ATTN_REP, ATTN_HEAD_DIM)
        sc = jnp.einsum("bqkrd,bskd->bkrqs", qcg, kc).astype(jnp.float32) * scale
        ssk = jnp.broadcast_to(sink32[None, :, :, None, None], sc.shape[:-1] + (1,))
        pcp = jax.nn.softmax(jnp.concatenate([sc, ssk], axis=-1), axis=-1)[..., :n_ctx].astype(vc.dtype)
        oc = jnp.einsum("bkrqs,bskd->bqkrd", pcp, vc)
        out_c = oc.reshape(b, n_ctx, ATTN_WIDTH) * jax.nn.silu(zc)
    return out_c, out_l


def hybrid_layer(x, ctx, silu_c, silu_cc, cos, sin, norm_w, w_mod, b_mod, w_in, ssd_conv_w, ssd_conv_b,
                 ssd_dt_bias, ssd_a_log, ssd_d, ssd_norm_w, sc_conv_w, attn_sink, w_out, with_ctx):
    shift, scale, gate = jnp.split(silu_c @ w_mod + b_mod, 3, axis=-1)
    shift_c, scale_c, gate_c = jnp.split(silu_cc @ w_mod + b_mod, 3, axis=-1)
    h = rmsnorm(x, norm_w) * (1 + scale[:, None]) + shift[:, None]
    hc = rmsnorm(ctx, norm_w) * (1 + scale_c) + shift_c
    pl, pc = h @ w_in, hc @ w_in
    ssd_l, sc_l, at_l = jnp.split(pl, [SSD_COLS, SSD_COLS + SC_COLS], axis=-1)
    ssd_c, sc_c, at_c = jnp.split(pc, [SSD_COLS, SSD_COLS + SC_COLS], axis=-1)
    ys_c, ys_l = ssd_branch(ssd_c, ssd_l, ssd_conv_w, ssd_conv_b, ssd_dt_bias, ssd_a_log, ssd_d,
                            ssd_norm_w, with_ctx)
    yc_c, yc_l = shortconv_branch(sc_c, sc_l, sc_conv_w, with_ctx)
    ya_c, ya_l = attn_branch(at_c, at_l, attn_sink, cos, sin, with_ctx)
    x = x + gate[:, None] * (jnp.concatenate([ys_l, yc_l, ya_l], axis=-1) @ w_out)
    if with_ctx:
        ctx = ctx + gate_c * (jnp.concatenate([ys_c, yc_c, ya_c], axis=-1) @ w_out)
    return x, ctx


def setup_inputs(seed: int = 0) -> dict:
    key = jax.random.key(seed)
    ks = jax.random.split(key, 20)
    nrm = jax.random.normal
    f32 = jnp.float32
    dt0 = jnp.exp(jax.random.uniform(ks[9], (DEPTH, 2, SSD_HEADS), f32, np.log(1e-3), np.log(1e-1)))
    return {
        "x": nrm(ks[0], (BATCH, SEQ, D_MODEL), f32),
        "c": nrm(ks[1], (BATCH, D_MODEL), f32),
        "ctx": nrm(ks[2], (BATCH, CTX_LEN, D_MODEL), f32),
        "c_ctx": nrm(ks[3], (D_MODEL,), f32),
        "norm_w": 1.0 + 0.02 * nrm(ks[4], (DEPTH, D_MODEL), f32),
        "w_mod": 0.5 * D_MODEL ** -0.5 * nrm(ks[5], (DEPTH, D_MODEL, 3 * D_MODEL), f32),
        "b_mod": 0.02 * nrm(ks[6], (DEPTH, 3 * D_MODEL), f32),
        "w_in": D_MODEL ** -0.5 * nrm(ks[7], (DEPTH, D_MODEL, IN_COLS), f32),
        "ssd_conv_w": SSD_CONV ** -0.5 * nrm(ks[8], (DEPTH, SSD_CONV, SSD_XBC), f32),
        "ssd_conv_b": 0.02 * nrm(ks[10], (DEPTH, SSD_XBC), f32),
        "ssd_dt_bias": dt0 + jnp.log(-jnp.expm1(-dt0)),
        "ssd_a_log": jnp.log(jax.random.uniform(ks[11], (DEPTH, 2, SSD_HEADS), f32, 1.0, 16.0)),
        "ssd_d": 1.0 + 0.02 * nrm(ks[12], (DEPTH, SSD_HEADS), f32),
        "ssd_norm_w": 1.0 + 0.02 * nrm(ks[13], (DEPTH, SSD_WIDTH), f32),
        "sc_conv_w": SC_CONV ** -0.5 * nrm(ks[14], (DEPTH, SC_CONV, SC_WIDTH), f32),
        "attn_sink": 0.5 * nrm(ks[15], (DEPTH, ATTN_HEADS), f32),
        "w_out": MIX_WIDTH ** -0.5 * nrm(ks[16], (DEPTH, MIX_WIDTH, D_MODEL), f32),
        "final_norm_w": 1.0 + 0.02 * nrm(ks[17], (D_MODEL,), f32),
    }


def reference(x, c, ctx, c_ctx, norm_w, w_mod, b_mod, w_in, ssd_conv_w, ssd_conv_b, ssd_dt_bias,
              ssd_a_log, ssd_d, ssd_norm_w, sc_conv_w, attn_sink, w_out, final_norm_w):
    cos, sin = rope_tables(x.shape[1])
    silu_c, silu_cc = jax.nn.silu(c), jax.nn.silu(c_ctx)
    for l in range(DEPTH):
        x, ctx = hybrid_layer(x, ctx, silu_c, silu_cc, cos, sin, norm_w[l], w_mod[l], b_mod[l], w_in[l],
                              ssd_conv_w[l], ssd_conv_b[l], ssd_dt_bias[l], ssd_a_log[l], ssd_d[l],
                              ssd_norm_w[l], sc_conv_w[l], attn_sink[l], w_out[l],
                              with_ctx=(l < DEPTH - 1))
    return rmsnorm(x, final_norm_w)
```

```python
import functools

import numpy as np
import jax
import jax.numpy as jnp
from jax import lax
from jax.experimental import pallas as pl
from jax.experimental.pallas import tpu as pltpu

F32 = jnp.float32
BF16 = jnp.bfloat16

EPS = 1e-6
NEG_INF = -1e30
ROPE_BASE = 10000.0
GRID_W = 64

D_MODEL = 1024
CHUNK = 128
HALO = 8
ROW_TILE = 256
SSD_WIDTH = 1024
SSD_HEADS = 16
SSD_GROUPS = 2
SSD_STATE = 128
GROUP_W = SSD_WIDTH // SSD_GROUPS
SC_WIDTH = 512
ATTN_WIDTH = 512
ATTN_HEADS = 8
HEAD_DIM = 64
MIX_WIDTH = 2048
LANES = 128

COL_Z = 0
COL_X = 1024
COL_SC = 2048
COL_QZ = 4096
COL_BC = 5120
COL_KV = 5632
COL_DT = 5888
PW = 6016

VMEM_LIMIT = 56 * 1024 * 1024


def _sigmoid(v):
    return 1.0 / (1.0 + jnp.exp(-v))


def _silu(v):
    return v * _sigmoid(v)


def _softplus(v):
    return jnp.maximum(v, 0.0) + jnp.log1p(jnp.exp(-jnp.abs(v)))


def _split2(v):
    hi = v.astype(BF16)
    lo = (v - hi.astype(F32)).astype(BF16)
    return hi, lo


def _split3(v):
    hi = v.astype(BF16)
    r = v - hi.astype(F32)
    mid = r.astype(BF16)
    lo = (r - mid.astype(F32)).astype(BF16)
    return hi, mid, lo


def _dot(a, b):
    return jnp.dot(a, b, preferred_element_type=F32)


def _dot_nt(a, b):
    return lax.dot_general(a, b, (((1,), (1,)), ((), ())), preferred_element_type=F32)


def _mod_kernel(c_ref, w_ref, b_ref, o_ref):
    sc = _silu(c_ref[...])
    o_ref[...] = _dot(sc.astype(BF16), w_ref[...].astype(BF16)) + b_ref[...]


def _modulation(cvec, w_mod, b_mod):
    depth = w_mod.shape[0]
    nblk = w_mod.shape[2] // D_MODEL
    return pl.pallas_call(
        _mod_kernel,
        grid=(depth, nblk),
        in_specs=[
            pl.BlockSpec((8, D_MODEL), lambda l, j: (0, 0)),
            pl.BlockSpec((None, D_MODEL, D_MODEL), lambda l, j: (l, 0, j)),
            pl.BlockSpec((None, 1, D_MODEL), lambda l, j: (l, 0, j)),
        ],
        out_specs=pl.BlockSpec((None, 8, D_MODEL), lambda l, j: (l, 0, j)),
        out_shape=jax.ShapeDtypeStruct((depth, 8, w_mod.shape[2]), F32),
        compiler_params=pltpu.CompilerParams(vmem_limit_bytes=VMEM_LIMIT),
        name="modulation",
    )(cvec, w_mod, b_mod.reshape(depth, 1, -1))


_IN_COL_CHUNK = 512


def _inproj_kernel(x_ref, mod_ref, nw_ref, w_ref, o_ref):
    x = x_ref[...]
    ms = jnp.mean(x * x, axis=-1, keepdims=True)
    h = x * lax.rsqrt(ms + EPS) * nw_ref[...]
    h = h * (1.0 + mod_ref[1:2, :]) + mod_ref[0:1, :]
    hb = h.astype(BF16)
    for c0 in range(0, PW, _IN_COL_CHUNK):
        c1 = min(c0 + _IN_COL_CHUNK, PW)
        o_ref[:, c0:c1] = _dot(hb, w_ref[:, c0:c1])


def _inproj(xa, modsel, norm_w, w_in_p, n_ctx_tiles):
    b, t, d = xa.shape
    return pl.pallas_call(
        _inproj_kernel,
        grid=(b, t // ROW_TILE),
        in_specs=[
            pl.BlockSpec((None, ROW_TILE, d), lambda bi, i: (bi, i, 0)),
            pl.BlockSpec((None, None, 3, d),
                         lambda bi, i: (bi, jnp.where(i >= n_ctx_tiles, 1, 0), 0, 0)),
            pl.BlockSpec((1, d), lambda bi, i: (0, 0)),
            pl.BlockSpec((d, PW), lambda bi, i: (0, 0), pipeline_mode=pl.Buffered(1)),
        ],
        out_specs=pl.BlockSpec((None, ROW_TILE, PW), lambda bi, i: (bi, i, 0)),
        out_shape=jax.ShapeDtypeStruct((b, t, PW), F32),
        compiler_params=pltpu.CompilerParams(
            dimension_semantics=("parallel", "parallel"), vmem_limit_bytes=VMEM_LIMIT),
        name="in_projection",
    )(xa, modsel, norm_w.reshape(1, d), w_in_p)


def _conv3(t, prev_row, next_row, w_ref):
    rows = lax.broadcasted_iota(jnp.int32, t.shape, 0)
    tp = jnp.where(rows == 0, prev_row, pltpu.roll(t, 1, 0))
    tn = jnp.where(rows == CHUNK - 1, next_row, pltpu.roll(t, CHUNK - 1, 0))
    return w_ref[0:1, :] * tp + w_ref[1:2, :] * t + w_ref[2:3, :] * tn


def _seq_edges(c, n_ctx_chunks, n_chunks):
    has_prev = jnp.logical_and(c != 0, c != n_ctx_chunks).astype(F32)
    has_next = jnp.logical_and(c != n_ctx_chunks - 1, c != n_chunks - 1).astype(F32)
    return has_prev, has_next


def _scan_chunk(step, direction, n_ctx_chunks, n_chunks):
    if direction == 0:
        return step
    return jnp.where(step < n_ctx_chunks, n_ctx_chunks - 1 - step,
                     n_chunks - 1 - (step - n_ctx_chunks))


def _ssd_kernel(direction, final, n_ctx_chunks, n_chunks, *refs):
    if final:
        (x_ref, xp_ref, xn_ref, bc_ref, bcp_ref, bcn_ref, dt_ref, cwx_ref, cbx_ref, cwbc_ref,
         cbbc_ref, dtb_ref, alog_ref, e_ref, z_ref, yf_ref, dsk_ref, nw_ref, o_ref,
         state_ref) = refs
    else:
        (x_ref, xp_ref, xn_ref, bc_ref, bcp_ref, bcn_ref, dt_ref, cwx_ref, cbx_ref, cwbc_ref,
         cbbc_ref, dtb_ref, alog_ref, e_ref, o_ref, state_ref) = refs

    step = pl.program_id(1)
    c = _scan_chunk(step, direction, n_ctx_chunks, n_chunks)

    @pl.when(step == 0)
    def _():
        state_ref[...] = jnp.zeros_like(state_ref)

    has_prev, has_next = _seq_edges(c, n_ctx_chunks, n_chunks)
    xc = _conv3(x_ref[...], xp_ref[HALO - 1:HALO, :] * has_prev, xn_ref[0:1, :] * has_next,
                cwx_ref) + cbx_ref[...]
    xc = _silu(xc)
    bc = _conv3(bc_ref[...], bcp_ref[HALO - 1:HALO, :] * has_prev, bcn_ref[0:1, :] * has_next,
                cwbc_ref) + cbbc_ref[...]
    bc = _silu(bc)

    lane = lax.broadcasted_iota(jnp.int32, (1, LANES), 1)
    mine = jnp.logical_and(lane >= SSD_HEADS * direction, lane < SSD_HEADS * (direction + 1))
    a_row = jnp.where(mine, -jnp.exp(alog_ref[...]), 0.0)
    dt = _softplus(dt_ref[...] + dtb_ref[...])
    da = dt * a_row

    r_i = lax.broadcasted_iota(jnp.int32, (CHUNK, CHUNK), 0)
    c_i = lax.broadcasted_iota(jnp.int32, (CHUNK, CHUNK), 1)
    tri = (c_i <= r_i) if direction == 0 else (c_i >= r_i)
    tri_b = jnp.where(tri, 1.0, 0.0).astype(BF16)
    u3 = _dot(tri_b, jnp.concatenate(_split3(da), axis=1))
    u = u3[:, 0:LANES] + u3[:, LANES:2 * LANES] + u3[:, 2 * LANES:3 * LANES]
    u_t = u.T
    tot = u[CHUNK - 1:CHUNK, :] if direction == 0 else u[0:1, :]
    tot_row = CHUNK - 1 if direction == 0 else 0
    exp_u = jnp.exp(u)
    exp_d = jnp.exp(tot - u)

    stack = jnp.concatenate(_split2(dt) + _split2(exp_u) + _split2(exp_d), axis=0)
    ex = _dot(stack, e_ref[...])
    dt_e = ex[0:CHUNK] + ex[CHUNK:2 * CHUNK]
    eu_e = ex[2 * CHUNK:3 * CHUNK] + ex[3 * CHUNK:4 * CHUNK]
    ed_e = ex[4 * CHUNK:5 * CHUNK] + ex[5 * CHUNK:6 * CHUNK]

    xdt = xc * dt_e
    xdec = xdt * ed_e
    half = lax.broadcasted_iota(jnp.int32, (CHUNK, LANES), 1) < HEAD_DIM

    y_slabs = []
    for g in range(SSD_GROUPS):
        b_g = bc[:, g * SSD_STATE:(g + 1) * SSD_STATE]
        c_g = bc[:, (SSD_GROUPS + g) * SSD_STATE:(SSD_GROUPS + g + 1) * SSD_STATE]
        cb = _dot_nt(c_g.astype(BF16), b_g.astype(BF16))
        st = state_ref[g]
        y_off = _dot(c_g.astype(BF16), st.astype(BF16)) * eu_e[:, g * GROUP_W:(g + 1) * GROUP_W]
        for j in range(GROUP_W // LANES):
            ms = []
            for hh in range(2):
                hl = SSD_HEADS * direction + g * (SSD_HEADS // SSD_GROUPS) + 2 * j + hh
                diff = u[:, hl:hl + 1] - u_t[hl:hl + 1, :]
                ms.append(cb * jnp.exp(jnp.where(tri, diff, -jnp.inf)))
            m2 = jnp.concatenate(ms, axis=1).astype(BF16)
            lo = g * GROUP_W + j * LANES
            slab = xdt[:, lo:lo + LANES]
            xbd = jnp.concatenate([jnp.where(half, slab, 0.0), jnp.where(half, 0.0, slab)],
                                  axis=0).astype(BF16)
            y_slabs.append(_dot(m2, xbd) + y_off[:, j * LANES:(j + 1) * LANES])
        decay = eu_e[tot_row:tot_row + 1, g * GROUP_W:(g + 1) * GROUP_W]
        state_ref[g] = st * decay + _dot(b_g.T.astype(BF16),
                                         xdec[:, g * GROUP_W:(g + 1) * GROUP_W].astype(BF16))
    y = jnp.concatenate(y_slabs, axis=1)

    if not final:
        o_ref[...] = y
    else:
        y = y + yf_ref[...] + xc * dsk_ref[...]
        gated = y * _silu(z_ref[...])
        outs = []
        for g in range(SSD_GROUPS):
            gg = gated[:, g * GROUP_W:(g + 1) * GROUP_W]
            ms = jnp.mean(gg * gg, axis=-1, keepdims=True)
            outs.append(gg * lax.rsqrt(ms + EPS))
        o_ref[...] = jnp.concatenate(outs, axis=1) * nw_ref[...]


def _ssd_pass(direction, p, consts, n_ctx_chunks, extra=None):
    b, t, _ = p.shape
    n_chunks = t // CHUNK
    n_halo = t // HALO
    per = CHUNK // HALO
    final = extra is not None

    def cidx(s):
        return _scan_chunk(s, direction, n_ctx_chunks, n_chunks)

    def main(col_blk):
        return lambda bi, s: (bi, cidx(s), col_blk)

    def prev(col_blk):
        return lambda bi, s: (bi, jnp.maximum(cidx(s) * per - 1, 0), col_blk)

    def nxt(col_blk):
        return lambda bi, s: (bi, jnp.minimum((cidx(s) + 1) * per, n_halo - 1), col_blk)

    def const(shape):
        return pl.BlockSpec(shape, lambda bi, s: (0,) * len(shape))

    xw, bcw = SSD_WIDTH, 2 * SSD_GROUPS * SSD_STATE
    in_specs = [
        pl.BlockSpec((None, CHUNK, xw), main(COL_X // xw)),
        pl.BlockSpec((None, HALO, xw), prev(COL_X // xw)),
        pl.BlockSpec((None, HALO, xw), nxt(COL_X // xw)),
        pl.BlockSpec((None, CHUNK, bcw), main(COL_BC // bcw)),
        pl.BlockSpec((None, HALO, bcw), prev(COL_BC // bcw)),
        pl.BlockSpec((None, HALO, bcw), nxt(COL_BC // bcw)),
        pl.BlockSpec((None, CHUNK, LANES), main(COL_DT // LANES)),
        const((3, xw)), const((1, xw)), const((3, bcw)), const((1, bcw)),
        pl.BlockSpec((None, 1, LANES), lambda bi, s: (direction, 0, 0)),
        pl.BlockSpec((None, 1, LANES), lambda bi, s: (direction, 0, 0)),
        pl.BlockSpec((None, LANES, xw), lambda bi, s: (direction, 0, 0)),
    ]
    args = [p, p, p, p, p, p, p, consts["cw_x"], consts["cb_x"], consts["cw_bc"], consts["cb_bc"],
            consts["dt_bias"], consts["a_log"], consts["expand"]]
    if final:
        in_specs += [
            pl.BlockSpec((None, CHUNK, xw), main(COL_Z // xw)),
            pl.BlockSpec((None, CHUNK, xw), main(0)),
            const((1, xw)), const((1, xw)),
        ]
        args += [p, extra, consts["d_skip"], consts["ssd_norm_w"]]
    return pl.pallas_call(
        functools.partial(_ssd_kernel, direction, final, n_ctx_chunks, n_chunks),
        grid=(b, n_chunks),
        in_specs=in_specs,
        out_specs=pl.BlockSpec((None, CHUNK, xw), main(0)),
        out_shape=jax.ShapeDtypeStruct((b, t, xw), F32),
        scratch_shapes=[pltpu.VMEM((SSD_GROUPS, SSD_STATE, GROUP_W), F32)],
        compiler_params=pltpu.CompilerParams(
            dimension_semantics=("parallel", "arbitrary"), vmem_limit_bytes=VMEM_LIMIT),
        name="ssd_backward" if final else "ssd_forward",
    )(*args)


def _rope(t, cos, sin_signed):
    lane = lax.broadcasted_iota(jnp.int32, t.shape, 1)
    first = (lane % 32) < 16
    swapped = jnp.where(first, pltpu.roll(t, LANES - 16, 1), pltpu.roll(t, 16, 1))
    return t * cos + swapped * sin_signed


def _attend(q_slabs, k_all, v_all, valid, sink_ref):
    lane = lax.broadcasted_iota(jnp.int32, (CHUNK, LANES), 1)
    left = lane < HEAD_DIM
    lane_kv = lax.broadcasted_iota(jnp.int32, v_all.shape, 1) < HEAD_DIM
    v_roll = pltpu.roll(v_all, HEAD_DIM, 1)
    k_b = k_all.astype(BF16)
    outs = []
    for j, qs in enumerate(q_slabs):
        kvh = j // 2
        qr = pltpu.roll(qs, HEAD_DIM, 1)
        if kvh == 0:
            q2 = jnp.concatenate([jnp.where(left, qs, 0.0), jnp.where(left, qr, 0.0)], axis=0)
            v_dup = jnp.where(lane_kv, v_all, v_roll)
        else:
            q2 = jnp.concatenate([jnp.where(left, 0.0, qr), jnp.where(left, 0.0, qs)], axis=0)
            v_dup = jnp.where(lane_kv, v_roll, v_all)
        s = _dot_nt(q2.astype(BF16), k_b)
        if valid is not None:
            s = jnp.where(valid, s, NEG_INF)
        sk = jnp.concatenate([
            jnp.broadcast_to(sink_ref[2 * j:2 * j + 1, 0:1], (CHUNK, 1)),
            jnp.broadcast_to(sink_ref[2 * j + 1:2 * j + 2, 0:1], (CHUNK, 1))], axis=0)
        m = jnp.maximum(jnp.max(s, axis=-1, keepdims=True), sk)
        p = jnp.exp(s - m)
        denom = jnp.sum(p, axis=-1, keepdims=True) + jnp.exp(sk - m)
        o = _dot(p.astype(BF16), v_dup.astype(BF16)) / denom
        outs.append(jnp.where(left, o[0:CHUNK], o[CHUNK:2 * CHUNK]))
    return outs


def _local_kernel(n_ctx_chunks, n_chunks, chunk_offset,
                  sc_ref, scp_ref, scn_ref, qz_ref, kv_ref, kvp_ref, kvn_ref, kvc_ref,
                  cs_ref, csp_ref, csn_ref, cw_ref, sink_ref, o_ref):
    c = pl.program_id(1) + chunk_offset
    has_prev, has_next = _seq_edges(c, n_ctx_chunks, n_chunks)

    sc = sc_ref[...]
    w = SC_WIDTH
    cv = sc[:, 0:w] * sc[:, w:2 * w]
    cv_prev = scp_ref[HALO - 1:HALO, 0:w] * scp_ref[HALO - 1:HALO, w:2 * w] * has_prev
    cv_next = scn_ref[0:1, 0:w] * scn_ref[0:1, w:2 * w] * has_next
    o_ref[:, 0:w] = sc[:, 2 * w:3 * w] * _conv3(cv, cv_prev, cv_next, cw_ref) * _silu(sc[:, 3 * w:4 * w])

    scale = HEAD_DIM ** -0.5
    q = qz_ref[:, 0:ATTN_WIDTH]
    gate = _silu(qz_ref[:, ATTN_WIDTH:2 * ATTN_WIDTH])
    kvc = kvc_ref[...]
    n_ctx = kvc.shape[0]

    def finish(slabs):
        o_ref[:, w:w + ATTN_WIDTH] = jnp.concatenate(slabs, axis=1) * gate

    @pl.when(c < n_ctx_chunks)
    def _():
        q_slabs = [q[:, i * LANES:(i + 1) * LANES] * scale for i in range(ATTN_WIDTH // LANES)]
        finish(_attend(q_slabs, kvc[:, 0:LANES], kvc[:, LANES:2 * LANES], None, sink_ref))

    @pl.when(c >= n_ctx_chunks)
    def _():
        cos, sin = cs_ref[:, 0:LANES], cs_ref[:, LANES:2 * LANES]
        q_slabs = [_rope(q[:, i * LANES:(i + 1) * LANES], cos, sin) * scale
                   for i in range(ATTN_WIDTH // LANES)]
        k_all = jnp.concatenate([
            _rope(kvp_ref[:, 0:LANES], csp_ref[:, 0:LANES], csp_ref[:, LANES:2 * LANES]),
            _rope(kv_ref[:, 0:LANES], cos, sin),
            _rope(kvn_ref[:, 0:LANES], csn_ref[:, 0:LANES], csn_ref[:, LANES:2 * LANES]),
            kvc[:, 0:LANES]], axis=0)
        v_all = jnp.concatenate([kvp_ref[:, LANES:2 * LANES], kv_ref[:, LANES:2 * LANES],
                                 kvn_ref[:, LANES:2 * LANES], kvc[:, LANES:2 * LANES]], axis=0)
        n_keys = 3 * CHUNK + n_ctx
        row = lax.broadcasted_iota(jnp.int32, (2 * CHUNK, n_keys), 0) % CHUNK
        col = lax.broadcasted_iota(jnp.int32, (2 * CHUNK, n_keys), 1)
        kpos = (c - 1) * CHUNK + col
        band = jnp.logical_and(col >= row, col <= row + 2 * CHUNK)
        inside = jnp.logical_and(kpos >= n_ctx_chunks * CHUNK, kpos < n_chunks * CHUNK)
        valid = jnp.logical_or(col >= 3 * CHUNK, jnp.logical_and(band, inside))
        finish(_attend(q_slabs, k_all, v_all, valid, sink_ref))


def _local_mixers(p, rope_tab, sc_conv_w, sink_b, n_ctx_chunks, skip_ctx):
    b, t, _ = p.shape
    n_chunks = t // CHUNK
    n_halo = t // HALO
    per = CHUNK // HALO
    off = n_ctx_chunks if skip_ctx else 0
    scw, qzw, kvw, hw = 4 * SC_WIDTH, 2 * ATTN_WIDTH, 2 * LANES, 2 * SC_WIDTH

    def main(col_blk):
        return lambda bi, i: (bi, i + off, col_blk)

    def lo(i):
        return jnp.maximum(i + off - 1, 0)

    def hi(i):
        return jnp.minimum(i + off + 1, n_chunks - 1)

    in_specs = [
        pl.BlockSpec((None, CHUNK, scw), main(COL_SC // scw)),
        pl.BlockSpec((None, HALO, hw),
                     lambda bi, i: (bi, jnp.maximum((i + off) * per - 1, 0), COL_SC // hw)),
        pl.BlockSpec((None, HALO, hw),
                     lambda bi, i: (bi, jnp.minimum((i + off + 1) * per, n_halo - 1), COL_SC // hw)),
        pl.BlockSpec((None, CHUNK, qzw), main(COL_QZ // qzw)),
        pl.BlockSpec((None, CHUNK, kvw), main(COL_KV // kvw)),
        pl.BlockSpec((None, CHUNK, kvw), lambda bi, i: (bi, lo(i), COL_KV // kvw)),
        pl.BlockSpec((None, CHUNK, kvw), lambda bi, i: (bi, hi(i), COL_KV // kvw)),
        pl.BlockSpec((None, n_ctx_chunks * CHUNK, kvw), lambda bi, i: (bi, 0, COL_KV // kvw)),
        pl.BlockSpec((CHUNK, 2 * LANES), lambda bi, i: (i + off, 0)),
        pl.BlockSpec((CHUNK, 2 * LANES), lambda bi, i: (lo(i), 0)),
        pl.BlockSpec((CHUNK, 2 * LANES), lambda bi, i: (hi(i), 0)),
        pl.BlockSpec((3, SC_WIDTH), lambda bi, i: (0, 0)),
        pl.BlockSpec((ATTN_HEADS, LANES), lambda bi, i: (0, 0)),
    ]
    return pl.pallas_call(
        functools.partial(_local_kernel, n_ctx_chunks, n_chunks, off),
        grid=(b, n_chunks - off),
        in_specs=in_specs,
        out_specs=pl.BlockSpec((None, CHUNK, SC_WIDTH + ATTN_WIDTH), main(0)),
        out_shape=jax.ShapeDtypeStruct((b, t, SC_WIDTH + ATTN_WIDTH), F32),
        compiler_params=pltpu.CompilerParams(
            dimension_semantics=("parallel", "parallel"), vmem_limit_bytes=VMEM_LIMIT),
        name="local_mixers",
    )(p, p, p, p, p, p, p, p, rope_tab, rope_tab, rope_tab, sc_conv_w, sink_b)


def _outproj_kernel(final, *refs):
    if final:
        ys_ref, yl_ref, x_ref, mod_ref, w_ref, fw_ref, o_ref = refs
    else:
        ys_ref, yl_ref, x_ref, mod_ref, w_ref, o_ref = refs
    acc = _dot(ys_ref[...].astype(BF16), w_ref[0:SSD_WIDTH, :])
    acc = acc + _dot(yl_ref[...].astype(BF16), w_ref[SSD_WIDTH:MIX_WIDTH, :])
    xn = x_ref[...] + mod_ref[2:3, :] * acc
    if final:
        ms = jnp.mean(xn * xn, axis=-1, keepdims=True)
        xn = xn * lax.rsqrt(ms + EPS) * fw_ref[...]
    o_ref[...] = xn


def _outproj(ys, yl, xa, modsel, w_out_b, n_ctx_tiles, final_norm_w=None):
    b, t, d = xa.shape
    final = final_norm_w is not None
    off = n_ctx_tiles if final else 0
    n_tiles = t // ROW_TILE - off

    def rows(bi, i):
        return (bi, i + off, 0)

    in_specs = [
        pl.BlockSpec((None, ROW_TILE, SSD_WIDTH), rows),
        pl.BlockSpec((None, ROW_TILE, SC_WIDTH + ATTN_WIDTH), rows),
        pl.BlockSpec((None, ROW_TILE, d), rows),
        pl.BlockSpec((None, None, 3, d),
                     lambda bi, i: (bi, jnp.where(i + off >= n_ctx_tiles, 1, 0), 0, 0)),
        pl.BlockSpec((MIX_WIDTH, d), lambda bi, i: (0, 0)),
    ]
    args = [ys, yl, xa, modsel, w_out_b]
    if final:
        in_specs.append(pl.BlockSpec((1, d), lambda bi, i: (0, 0)))
        args.append(final_norm_w.reshape(1, d))
    return pl.pallas_call(
        functools.partial(_outproj_kernel, final),
        grid=(b, n_tiles),
        in_specs=in_specs,
        out_specs=pl.BlockSpec((None, ROW_TILE, d), lambda bi, i: (bi, i, 0)),
        out_shape=jax.ShapeDtypeStruct((b, n_tiles * ROW_TILE, d), F32),
        compiler_params=pltpu.CompilerParams(
            dimension_semantics=("parallel", "parallel"), vmem_limit_bytes=VMEM_LIMIT),
        name="out_projection_final" if final else "out_projection",
    )(*args)


def _permute_in_columns(w):
    d = w.shape[0]
    ssd_cols = 2592
    sc0 = ssd_cols
    at0 = sc0 + 4 * SC_WIDTH
    parts = [
        w[:, 0:2048],
        w[:, sc0:at0],
        w[:, at0:at0 + 512],
        w[:, at0 + 768:at0 + 1280],
        w[:, 2048:2560],
        w[:, at0 + 512:at0 + 768],
        w[:, 2560:2592],
        jnp.zeros((d, PW - COL_DT - 2 * SSD_HEADS), w.dtype),
    ]
    return jnp.concatenate(parts, axis=1)


def _rope_table(n_ctx, n_lat):
    n_rows = n_lat // GRID_W
    rows = np.repeat(np.arange(n_rows), GRID_W).astype(np.float32)
    cols = np.tile(np.arange(GRID_W), n_rows).astype(np.float32)
    axis_dim = HEAD_DIM // 2
    inv_freq = jnp.asarray(ROPE_BASE, F32) ** (-jnp.arange(0, axis_dim, 2, dtype=F32) / axis_dim)
    ang = jnp.concatenate([jnp.asarray(rows)[:, None] * inv_freq,
                           jnp.asarray(cols)[:, None] * inv_freq], axis=-1)
    cos, sin = jnp.cos(ang), jnp.sin(ang)
    q = HEAD_DIM // 4
    cos_h = jnp.concatenate([cos[:, 0:q], cos[:, 0:q], cos[:, q:2 * q], cos[:, q:2 * q]], axis=1)
    sin_h = jnp.concatenate([-sin[:, 0:q], sin[:, 0:q], -sin[:, q:2 * q], sin[:, q:2 * q]], axis=1)
    tab = jnp.concatenate([cos_h, cos_h, sin_h, sin_h], axis=1)
    ident = jnp.concatenate([jnp.ones((n_ctx, LANES), F32), jnp.zeros((n_ctx, LANES), F32)], axis=1)
    return jnp.concatenate([ident, tab], axis=0)


def _expand_matrix():
    e = np.zeros((2, LANES, SSD_WIDTH), np.float32)
    for d in range(2):
        for h in range(SSD_HEADS):
            e[d, d * SSD_HEADS + h, h * HEAD_DIM:(h + 1) * HEAD_DIM] = 1.0
    return jnp.asarray(e, BF16)


def _dir_rows(v):
    out = jnp.zeros((2, 1, LANES), F32)
    for d in range(2):
        out = out.at[d, 0, d * SSD_HEADS:(d + 1) * SSD_HEADS].set(v[d])
    return out


def kernel(x, c, ctx, c_ctx, norm_w, w_mod, b_mod, w_in, ssd_conv_w, ssd_conv_b, ssd_dt_bias,
           ssd_a_log, ssd_d, ssd_norm_w, sc_conv_w, attn_sink, w_out, final_norm_w):
    b, n_lat, d = x.shape
    n_ctx = ctx.shape[1]
    depth = w_in.shape[0]
    assert d == D_MODEL and n_ctx % ROW_TILE == 0 and n_lat % ROW_TILE == 0 and b <= 7
    n_ctx_chunks = n_ctx // CHUNK
    n_ctx_tiles = n_ctx // ROW_TILE

    cvec = jnp.zeros((8, d), F32).at[0:b].set(c).at[b].set(c_ctx)
    mod = _modulation(cvec, w_mod, b_mod)
    mod = mod.reshape(depth, 8, 3, d)
    modsel = jnp.stack([jnp.broadcast_to(mod[:, b][:, None], (depth, b, 3, d)), mod[:, 0:b]], axis=2)

    xa = jnp.concatenate([ctx, x], axis=1)
    rope_tab = _rope_table(n_ctx, n_lat)
    expand = _expand_matrix()

    out = None
    for l in range(depth):
        last = l == depth - 1
        w_in_p = _permute_in_columns(w_in[l]).astype(BF16)
        cw = ssd_conv_w[l]
        cbias = ssd_conv_b[l]
        consts = {
            "cw_x": cw[:, 0:SSD_WIDTH], "cb_x": cbias[0:SSD_WIDTH].reshape(1, -1),
            "cw_bc": cw[:, SSD_WIDTH:], "cb_bc": cbias[SSD_WIDTH:].reshape(1, -1),
            "dt_bias": _dir_rows(ssd_dt_bias[l]), "a_log": _dir_rows(ssd_a_log[l]),
            "expand": expand,
            "d_skip": jnp.repeat(ssd_d[l], HEAD_DIM).reshape(1, -1),
            "ssd_norm_w": ssd_norm_w[l].reshape(1, -1),
        }
        sink_b = jnp.broadcast_to(attn_sink[l][:, None], (ATTN_HEADS, LANES))

        p = _inproj(xa, modsel[l], norm_w[l], w_in_p, n_ctx_tiles)
        yf = _ssd_pass(0, p, consts, n_ctx_chunks)
        ys = _ssd_pass(1, p, consts, n_ctx_chunks, extra=yf)
        yl = _local_mixers(p, rope_tab, sc_conv_w[l], sink_b, n_ctx_chunks, skip_ctx=False)
        w_out_b = w_out[l].astype(BF16)
        if last:
            out = _outproj(ys, yl, xa, modsel[l], w_out_b, n_ctx_tiles, final_norm_w)
        else:
            xa = _outproj(ys, yl, xa, modsel[l], w_out_b, n_ctx_tiles)
    return out
```

```python
import functools

import numpy as np
import jax
import jax.numpy as jnp
from jax import lax
from jax.experimental import pallas as pl
from jax.experimental.pallas import tpu as pltpu

F32 = jnp.float32
BF16 = jnp.bfloat16

EPS = 1e-6
NEG_INF = -1e30
ROPE_BASE = 10000.0
GRID_W = 64

D_MODEL = 1024
CHUNK = 128
HALO = 8
ROW_TILE = 256
SSD_WIDTH = 1024
SSD_HEADS = 16
SSD_GROUPS = 2
SSD_STATE = 128
GROUP_W = SSD_WIDTH // SSD_GROUPS
BC_WIDTH = 2 * SSD_GROUPS * SSD_STATE
SC_WIDTH = 512
ATTN_WIDTH = 512
ATTN_HEADS = 8
ATTN_REP = 4
HEAD_DIM = 64
MIX_WIDTH = 2048
LANES = 128

W_X = 0
W_BC = 1024
W_VC = 1536
W_ZS = 2560
W_BZ = 3584
W_Q = 4608
W_ZA = 5120
W_KV = 5632
W_DT = 5888
PW = 6016

F_XC = 0
F_ZS = 1024
F_YC = 2048
F_GA = 2560
F_DT = 3072
FW = 3200
B_Q = 0
B_BC = 1024
B_KV = 1536
BW = 1792

VMEM_LIMIT = 56 * 1024 * 1024


def _sigmoid(v):
    return 1.0 / (1.0 + jnp.exp(-v))


def _silu(v):
    return v * _sigmoid(v)


def _softplus(v):
    return jnp.maximum(v, 0.0) + jnp.log1p(jnp.exp(-jnp.abs(v)))


def _split2(v):
    hi = v.astype(BF16)
    lo = (v - hi.astype(F32)).astype(BF16)
    return hi, lo


def _split3(v):
    hi = v.astype(BF16)
    r = v - hi.astype(F32)
    mid = r.astype(BF16)
    lo = (r - mid.astype(F32)).astype(BF16)
    return hi, mid, lo


def _dot(a, b):
    return jnp.dot(a, b, preferred_element_type=F32)


def _dot_nt(a, b):
    return lax.dot_general(a, b, (((1,), (1,)), ((), ())), preferred_element_type=F32)


def _rmsnorm(x, w):
    ms = jnp.mean(x * x, axis=-1, keepdims=True)
    return x * lax.rsqrt(ms + EPS) * w


def _mod_kernel(c_ref, w_ref, b_ref, o_ref):
    sc = _silu(c_ref[...])
    o_ref[...] = _dot(sc.astype(BF16), w_ref[...].astype(BF16)) + b_ref[...]


def _modulation(cvec, w_mod, b_mod):
    depth = w_mod.shape[0]
    nblk = w_mod.shape[2] // D_MODEL
    return pl.pallas_call(
        _mod_kernel,
        grid=(depth, nblk),
        in_specs=[
            pl.BlockSpec((8, D_MODEL), lambda l, j: (0, 0)),
            pl.BlockSpec((None, D_MODEL, D_MODEL), lambda l, j: (l, 0, j)),
            pl.BlockSpec((None, 1, D_MODEL), lambda l, j: (l, 0, j)),
        ],
        out_specs=pl.BlockSpec((None, 8, D_MODEL), lambda l, j: (l, 0, j)),
        out_shape=jax.ShapeDtypeStruct((depth, 8, w_mod.shape[2]), F32),
        compiler_params=pltpu.CompilerParams(vmem_limit_bytes=VMEM_LIMIT),
        name="modulation",
    )(cvec, w_mod, b_mod.reshape(depth, 1, -1))


def _conv3(t, prev_row, next_row, w_ref):
    n = t.shape[0]
    rows = lax.broadcasted_iota(jnp.int32, t.shape, 0)
    tp = jnp.where(rows == 0, prev_row, pltpu.roll(t, 1, 0))
    tn = jnp.where(rows == n - 1, next_row, pltpu.roll(t, n - 1, 0))
    return w_ref[0:1, :] * tp + w_ref[1:2, :] * t + w_ref[2:3, :] * tn


def _rope(t, cos, sin_signed):
    lane = lax.broadcasted_iota(jnp.int32, t.shape, 1)
    first = (lane % 32) < 16
    swapped = jnp.where(first, pltpu.roll(t, LANES - 16, 1), pltpu.roll(t, 16, 1))
    return t * cos + swapped * sin_signed


def _inproj_kernel(n_ctx_tiles, n_tiles,
                   x_ref, xp_ref, xn_ref, mod_ref, nw_ref, w_ref, cwx_ref, cbx_ref, cwbc_ref,
                   cbbc_ref, cwsc_ref, rope_ref, f_ref, b_ref):
    i = pl.program_id(1)
    has_prev = jnp.logical_and(i != 0, i != n_ctx_tiles).astype(F32)
    has_next = jnp.logical_and(i != n_ctx_tiles - 1, i != n_tiles - 1).astype(F32)

    def norm_mod(x):
        return (_rmsnorm(x, nw_ref[...]) * (1.0 + mod_ref[1:2, :]) + mod_ref[0:1, :]).astype(BF16)

    h = norm_mod(x_ref[...])
    hh = jnp.concatenate([norm_mod(xp_ref[...]), norm_mod(xn_ref[...])], axis=0)

    def proj(col, width):
        return _dot(h, w_ref[:, col:col + width])

    def proj_halo(col, width):
        ph = _dot(hh, w_ref[:, col:col + width])
        return ph[HALO - 1:HALO, :] * has_prev, ph[HALO:HALO + 1, :] * has_next

    half = SSD_WIDTH // 2
    for k in range(2):
        pv, nx = proj_halo(W_X + k * half, half)
        y = _conv3(proj(W_X + k * half, half), pv, nx, cwx_ref.at[:, k * half:(k + 1) * half])
        f_ref[:, F_XC + k * half:F_XC + (k + 1) * half] = _silu(y + cbx_ref[:, k * half:(k + 1) * half])
    pv, nx = proj_halo(W_BC, BC_WIDTH)
    y = _conv3(proj(W_BC, BC_WIDTH), pv, nx, cwbc_ref) + cbbc_ref[...]
    b_ref[:, B_BC:B_BC + BC_WIDTH] = _silu(y).astype(BF16)

    w = SC_WIDTH
    vc = proj(W_VC, 2 * w)
    vch = _dot(hh, w_ref[:, W_VC:W_VC + 2 * w])
    cvh = vch[:, 0:w] * vch[:, w:2 * w]
    conv = _conv3(vc[:, 0:w] * vc[:, w:2 * w], cvh[HALO - 1:HALO, :] * has_prev,
                  cvh[HALO:HALO + 1, :] * has_next, cwsc_ref)
    bz = proj(W_BZ, 2 * w)
    f_ref[:, F_YC:F_YC + w] = bz[:, 0:w] * conv * _silu(bz[:, w:2 * w])

    for k in range(2):
        f_ref[:, F_ZS + k * half:F_ZS + (k + 1) * half] = _silu(proj(W_ZS + k * half, half))
    f_ref[:, F_GA:F_GA + ATTN_WIDTH] = _silu(proj(W_ZA, ATTN_WIDTH))
    f_ref[:, F_DT:F_DT + LANES] = proj(W_DT, LANES)

    cos, sin = rope_ref[:, 0:LANES], rope_ref[:, LANES:2 * LANES]
    left = lax.broadcasted_iota(jnp.int32, (ROW_TILE, LANES), 1) < HEAD_DIM
    q = proj(W_Q, ATTN_WIDTH)
    scale = HEAD_DIM ** -0.5
    for j in range(ATTN_WIDTH // LANES):
        qs = _rope(q[:, j * LANES:(j + 1) * LANES], cos, sin) * scale
        qr = pltpu.roll(qs, HEAD_DIM, 1)
        if j // 2 == 0:
            h0, h1 = jnp.where(left, qs, 0.0), jnp.where(left, qr, 0.0)
        else:
            h0, h1 = jnp.where(left, 0.0, qr), jnp.where(left, 0.0, qs)
        b_ref[:, B_Q + 2 * j * LANES:B_Q + (2 * j + 1) * LANES] = h0.astype(BF16)
        b_ref[:, B_Q + (2 * j + 1) * LANES:B_Q + (2 * j + 2) * LANES] = h1.astype(BF16)
    kv = proj(W_KV, 2 * LANES)
    b_ref[:, B_KV:B_KV + LANES] = _rope(kv[:, 0:LANES], cos, sin).astype(BF16)
    b_ref[:, B_KV + LANES:B_KV + 2 * LANES] = kv[:, LANES:2 * LANES].astype(BF16)


def _inproj(xa, modsel, norm_w, w_in_p, consts, rope_tab, n_ctx_tiles):
    b, t, d = xa.shape
    n_tiles = t // ROW_TILE
    per = ROW_TILE // HALO
    n_halo = t // HALO

    def const(shape):
        return pl.BlockSpec(shape, lambda bi, i: (0,) * len(shape))

    return pl.pallas_call(
        functools.partial(_inproj_kernel, n_ctx_tiles, n_tiles),
        grid=(b, n_tiles),
        in_specs=[
            pl.BlockSpec((None, ROW_TILE, d), lambda bi, i: (bi, i, 0)),
            pl.BlockSpec((None, HALO, d), lambda bi, i: (bi, jnp.maximum(i * per - 1, 0), 0)),
            pl.BlockSpec((None, HALO, d), lambda bi, i: (bi, jnp.minimum((i + 1) * per, n_halo - 1), 0)),
            pl.BlockSpec((None, None, 3, d),
                         lambda bi, i: (bi, jnp.where(i >= n_ctx_tiles, 1, 0), 0, 0)),
            const((1, d)),
            pl.BlockSpec((d, PW), lambda bi, i: (0, 0), pipeline_mode=pl.Buffered(1)),
            const((3, SSD_WIDTH)), const((1, SSD_WIDTH)), const((3, BC_WIDTH)), const((1, BC_WIDTH)),
            const((3, SC_WIDTH)),
            pl.BlockSpec((ROW_TILE, 2 * LANES), lambda bi, i: (i, 0)),
        ],
        out_specs=[
            pl.BlockSpec((None, ROW_TILE, FW), lambda bi, i: (bi, i, 0)),
            pl.BlockSpec((None, ROW_TILE, BW), lambda bi, i: (bi, i, 0)),
        ],
        out_shape=[jax.ShapeDtypeStruct((b, t, FW), F32), jax.ShapeDtypeStruct((b, t, BW), BF16)],
        compiler_params=pltpu.CompilerParams(
            dimension_semantics=("parallel", "parallel"), vmem_limit_bytes=VMEM_LIMIT),
        name="in_projection",
    )(xa, xa, xa, modsel, norm_w.reshape(1, d), w_in_p, consts["cw_x"], consts["cb_x"],
      consts["cw_bc"], consts["cb_bc"], consts["cw_sc"], rope_tab)


def _scan_chunk(step, direction, n_ctx_chunks, n_chunks):
    if direction == 0:
        return step
    return jnp.where(step < n_ctx_chunks, n_ctx_chunks - 1 - step,
                     n_chunks - 1 - (step - n_ctx_chunks))


def _ssd_prepare(direction, xc, dt_raw, dtb_ref, alog_ref, e_ref):
    lane = lax.broadcasted_iota(jnp.int32, (1, LANES), 1)
    mine = jnp.logical_and(lane >= SSD_HEADS * direction, lane < SSD_HEADS * (direction + 1))
    a_row = jnp.where(mine, -jnp.exp(alog_ref[...]), 0.0)
    dt = _softplus(dt_raw + dtb_ref[...])
    da = dt * a_row

    r_i = lax.broadcasted_iota(jnp.int32, (CHUNK, CHUNK), 0)
    c_i = lax.broadcasted_iota(jnp.int32, (CHUNK, CHUNK), 1)
    tri = (c_i <= r_i) if direction == 0 else (c_i >= r_i)
    tri_b = jnp.where(tri, 1.0, 0.0).astype(BF16)
    u3 = _dot(tri_b, jnp.concatenate(_split3(da), axis=1))
    u = u3[:, 0:LANES] + u3[:, LANES:2 * LANES] + u3[:, 2 * LANES:3 * LANES]
    tot_row = CHUNK - 1 if direction == 0 else 0
    tot = u[tot_row:tot_row + 1, :]
    exp_u = jnp.exp(u)
    exp_d = jnp.exp(tot - u)

    stack = jnp.concatenate((dt.astype(BF16), exp_d.astype(BF16)) + _split2(exp_u), axis=0)
    ex = _dot(stack, e_ref[...])
    xdt = xc * ex[0:CHUNK]
    xdec = xdt * ex[CHUNK:2 * CHUNK]
    eu_e = ex[2 * CHUNK:3 * CHUNK] + ex[3 * CHUNK:4 * CHUNK]
    return dict(direction=direction, tri=tri, u=u, u_t=u.T, xdt=xdt, xdec=xdec, eu_e=eu_e,
                tot_row=tot_row)


def _ssd_output(prep, bc, state_ref):
    direction, tri, u, u_t = prep["direction"], prep["tri"], prep["u"], prep["u_t"]
    xdt, eu_e = prep["xdt"], prep["eu_e"]
    half = lax.broadcasted_iota(jnp.int32, (CHUNK, LANES), 1) < HEAD_DIM
    y_slabs = []
    for g in range(SSD_GROUPS):
        b_g = bc[:, g * SSD_STATE:(g + 1) * SSD_STATE]
        c_g = bc[:, (SSD_GROUPS + g) * SSD_STATE:(SSD_GROUPS + g + 1) * SSD_STATE]
        cb = _dot_nt(c_g, b_g)
        y_off = _dot(c_g, state_ref[g].astype(BF16)) * eu_e[:, g * GROUP_W:(g + 1) * GROUP_W]
        for j in range(GROUP_W // LANES):
            ms = []
            for hh in range(2):
                hl = SSD_HEADS * direction + g * (SSD_HEADS // SSD_GROUPS) + 2 * j + hh
                diff = u[:, hl:hl + 1] - u_t[hl:hl + 1, :]
                ms.append(cb * jnp.exp(jnp.where(tri, diff, -jnp.inf)))
            m2 = jnp.concatenate(ms, axis=1).astype(BF16)
            lo = g * GROUP_W + j * LANES
            slab = xdt[:, lo:lo + LANES]
            xbd = jnp.concatenate([jnp.where(half, slab, 0.0), jnp.where(half, 0.0, slab)],
                                  axis=0).astype(BF16)
            y_slabs.append(_dot(m2, xbd) + y_off[:, j * LANES:(j + 1) * LANES])
    return jnp.concatenate(y_slabs, axis=1)


def _ssd_update_state(prep, bc, state_ref):
    eu_e, xdec, tot_row = prep["eu_e"], prep["xdec"], prep["tot_row"]
    for g in range(SSD_GROUPS):
        b_g = bc[:, g * SSD_STATE:(g + 1) * SSD_STATE]
        decay = eu_e[tot_row:tot_row + 1, g * GROUP_W:(g + 1) * GROUP_W]
        state_ref[g] = state_ref[g] * decay + _dot(
            b_g.astype(F32).T.astype(BF16), xdec[:, g * GROUP_W:(g + 1) * GROUP_W].astype(BF16))


def _ssd_fwd_kernel(xc_ref, bc_ref, dt_ref, dtb_ref, alog_ref, e_ref, o_ref, state_ref):
    @pl.when(pl.program_id(1) == 0)
    def _():
        state_ref[...] = jnp.zeros_like(state_ref)

    prep = _ssd_prepare(0, xc_ref[...], dt_ref[...], dtb_ref, alog_ref, e_ref)
    bc = bc_ref[...]
    o_ref[...] = _ssd_output(prep, bc, state_ref)
    _ssd_update_state(prep, bc, state_ref)


def _ssd_forward(pf, pb, consts):
    b, t, _ = pf.shape
    return pl.pallas_call(
        _ssd_fwd_kernel,
        grid=(b, t // CHUNK),
        in_specs=[
            pl.BlockSpec((None, CHUNK, SSD_WIDTH), lambda bi, s: (bi, s, F_XC // SSD_WIDTH)),
            pl.BlockSpec((None, CHUNK, BC_WIDTH), lambda bi, s: (bi, s, B_BC // BC_WIDTH)),
            pl.BlockSpec((None, CHUNK, LANES), lambda bi, s: (bi, s, F_DT // LANES)),
            pl.BlockSpec((None, 1, LANES), lambda bi, s: (0, 0, 0)),
            pl.BlockSpec((None, 1, LANES), lambda bi, s: (0, 0, 0)),
            pl.BlockSpec((None, LANES, SSD_WIDTH), lambda bi, s: (0, 0, 0)),
        ],
        out_specs=pl.BlockSpec((None, CHUNK, SSD_WIDTH), lambda bi, s: (bi, s, 0)),
        out_shape=jax.ShapeDtypeStruct((b, t, SSD_WIDTH), F32),
        scratch_shapes=[pltpu.VMEM((SSD_GROUPS, SSD_STATE, GROUP_W), F32)],
        compiler_params=pltpu.CompilerParams(
            dimension_semantics=("parallel", "arbitrary"), vmem_limit_bytes=VMEM_LIMIT),
        name="ssd_forward",
    )(pf, pb, pf, consts["dt_bias"], consts["a_log"], consts["expand"])


def _attend(q_ref, k_all, v_all, bias, sink_ref):
    left = lax.broadcasted_iota(jnp.int32, (CHUNK, LANES), 1) < HEAD_DIM
    outs = []
    for j in range(ATTN_HEADS // 2):
        q2 = jnp.concatenate([q_ref[:, 2 * j * LANES:(2 * j + 1) * LANES],
                              q_ref[:, (2 * j + 1) * LANES:(2 * j + 2) * LANES]], axis=0)
        s = _dot_nt(q2, k_all)
        if bias is not None:
            s = s + bias
        sk = jnp.concatenate([
            jnp.broadcast_to(sink_ref[2 * j:2 * j + 1, 0:1], (CHUNK, 1)),
            jnp.broadcast_to(sink_ref[2 * j + 1:2 * j + 2, 0:1], (CHUNK, 1))], axis=0)
        m = jnp.maximum(jnp.max(s, axis=-1, keepdims=True), sk)
        p = jnp.exp(s - m)
        denom = jnp.sum(p, axis=-1, keepdims=True) + jnp.exp(sk - m)
        o = _dot(p.astype(BF16), v_all) / denom
        o0, o1 = o[0:CHUNK], o[CHUNK:2 * CHUNK]
        if j // 2 == 0:
            outs.append(jnp.where(left, o0, pltpu.roll(o1, HEAD_DIM, 1)))
        else:
            outs.append(jnp.where(left, pltpu.roll(o0, HEAD_DIM, 1), o1))
    return outs


def _mix_bwd_kernel(final, n_ctx_chunks, n_chunks, *refs):
    (xc_ref, bc_ref, dt_ref, zs_ref, yf_ref, yc_ref, ga_ref, q_ref, kv_ref, kvp_ref, kvn_ref,
     kvc_ref, x_ref, mod_ref, w_ref, dtb_ref, alog_ref, e_ref, dsk_ref, nw_ref, sink_ref,
     bias_ref) = refs[:22]
    if final:
        fw_ref, o_ref, state_ref, ya_ref = refs[22:]
    else:
        o_ref, state_ref, ya_ref = refs[22:]

    step = pl.program_id(1)
    c = _scan_chunk(step, 1, n_ctx_chunks, n_chunks)

    @pl.when(step == 0)
    def _():
        state_ref[...] = jnp.zeros_like(state_ref)

    xc = xc_ref[...]
    bc = bc_ref[...]
    prep = _ssd_prepare(1, xc, dt_ref[...], dtb_ref, alog_ref, e_ref)

    def emit():
        y = _ssd_output(prep, bc, state_ref) + yf_ref[...] + xc * dsk_ref[...]
        gated = y * zs_ref[...]
        ys = []
        for g in range(SSD_GROUPS):
            gg = gated[:, g * GROUP_W:(g + 1) * GROUP_W]
            ms = jnp.mean(gg * gg, axis=-1, keepdims=True)
            ys.append(gg * lax.rsqrt(ms + EPS))
        ys = jnp.concatenate(ys, axis=1) * nw_ref[...]

        if not final:
            @pl.when(c < n_ctx_chunks)
            def _():
                ya_ref[...] = jnp.concatenate(
                    _attend(q_ref, kvc_ref[:, 0:LANES], kvc_ref[:, LANES:2 * LANES], None, sink_ref),
                    axis=1)

        @pl.when(c >= n_ctx_chunks)
        def _():
            k_all = jnp.concatenate([kvp_ref[:, 0:LANES], kv_ref[:, 0:LANES], kvn_ref[:, 0:LANES],
                                     kvc_ref[:, 0:LANES]], axis=0)
            v_all = jnp.concatenate([kvp_ref[:, LANES:2 * LANES], kv_ref[:, LANES:2 * LANES],
                                     kvn_ref[:, LANES:2 * LANES], kvc_ref[:, LANES:2 * LANES]], axis=0)
            col = lax.broadcasted_iota(jnp.int32, (1, bias_ref.shape[1]), 1)
            no_prev = jnp.where(c > n_ctx_chunks, 0.0, NEG_INF)
            no_next = jnp.where(c < n_chunks - 1, 0.0, NEG_INF)
            edge = jnp.where(col < CHUNK, no_prev,
                             jnp.where(jnp.logical_and(col >= 2 * CHUNK, col < 3 * CHUNK), no_next, 0.0))
            ya_ref[...] = jnp.concatenate(
                _attend(q_ref, k_all, v_all, bias_ref[...] + edge, sink_ref), axis=1)

        ya = ya_ref[...] * ga_ref[...]
        acc = _dot(ys.astype(BF16), w_ref[0:SSD_WIDTH, :])
        acc = acc + _dot(yc_ref[...].astype(BF16), w_ref[SSD_WIDTH:SSD_WIDTH + SC_WIDTH, :])
        acc = acc + _dot(ya.astype(BF16), w_ref[SSD_WIDTH + SC_WIDTH:MIX_WIDTH, :])
        xn = x_ref[...] + mod_ref[2:3, :] * acc
        if final:
            xn = _rmsnorm(xn, fw_ref[...])
        o_ref[...] = xn

    if final:
        pl.when(c >= n_ctx_chunks)(emit)
    else:
        emit()
    _ssd_update_state(prep, bc, state_ref)


def _mix_backward(pf, pb, yf, xa, modsel, w_out_b, consts, n_ctx_chunks, final_norm_w=None):
    b, t, d = xa.shape
    n_chunks = t // CHUNK
    final = final_norm_w is not None
    n_ctx = n_ctx_chunks * CHUNK

    def cidx(s):
        return _scan_chunk(s, 1, n_ctx_chunks, n_chunks)

    def blk(width, offset):
        return pl.BlockSpec((None, CHUNK, width), lambda bi, s: (bi, cidx(s), offset // width))

    def const(shape):
        return pl.BlockSpec(shape, lambda bi, s: (0,) * len(shape))

    kvw = 2 * LANES
    in_specs = [
        blk(SSD_WIDTH, F_XC), blk(BC_WIDTH, B_BC), blk(LANES, F_DT), blk(SSD_WIDTH, F_ZS),
        blk(SSD_WIDTH, 0),
        blk(SC_WIDTH, F_YC), blk(ATTN_WIDTH, F_GA), blk(2 * ATTN_WIDTH, B_Q),
        blk(kvw, B_KV),
        pl.BlockSpec((None, CHUNK, kvw),
                     lambda bi, s: (bi, jnp.maximum(cidx(s) - 1, 0), B_KV // kvw)),
        pl.BlockSpec((None, CHUNK, kvw),
                     lambda bi, s: (bi, jnp.minimum(cidx(s) + 1, n_chunks - 1), B_KV // kvw)),
        pl.BlockSpec((None, n_ctx, kvw), lambda bi, s: (bi, 0, B_KV // kvw)),
        blk(d, 0),
        pl.BlockSpec((None, None, 3, d),
                     lambda bi, s: (bi, jnp.where(cidx(s) >= n_ctx_chunks, 1, 0), 0, 0)),
        pl.BlockSpec((MIX_WIDTH, d), lambda bi, s: (0, 0), pipeline_mode=pl.Buffered(1)),
        pl.BlockSpec((None, 1, LANES), lambda bi, s: (1, 0, 0)),
        pl.BlockSpec((None, 1, LANES), lambda bi, s: (1, 0, 0)),
        pl.BlockSpec((None, LANES, SSD_WIDTH), lambda bi, s: (1, 0, 0)),
        const((1, SSD_WIDTH)), const((1, SSD_WIDTH)), const((ATTN_HEADS, LANES)),
        const((2 * CHUNK, 3 * CHUNK + n_ctx)),
    ]
    args = [pf, pb, pf, pf, yf, pf, pf, pb, pb, pb, pb, pb, xa, modsel, w_out_b,
            consts["dt_bias"], consts["a_log"], consts["expand"], consts["d_skip"],
            consts["ssd_norm_w"], consts["sink"], consts["band_bias"]]
    if final:
        in_specs.append(const((1, d)))
        args.append(final_norm_w.reshape(1, d))
        out_spec = pl.BlockSpec(
            (None, CHUNK, d),
            lambda bi, s: (bi, jnp.where(cidx(s) >= n_ctx_chunks, cidx(s), n_chunks - 1) - n_ctx_chunks, 0))
        out_shape = jax.ShapeDtypeStruct((b, t - n_ctx, d), F32)
    else:
        out_spec = blk(d, 0)
        out_shape = jax.ShapeDtypeStruct((b, t, d), F32)
    return pl.pallas_call(
        functools.partial(_mix_bwd_kernel, final, n_ctx_chunks, n_chunks),
        grid=(b, n_chunks),
        in_specs=in_specs,
        out_specs=out_spec,
        out_shape=out_shape,
        scratch_shapes=[pltpu.VMEM((SSD_GROUPS, SSD_STATE, GROUP_W), F32),
                        pltpu.VMEM((CHUNK, ATTN_WIDTH), F32)],
        compiler_params=pltpu.CompilerParams(
            dimension_semantics=("parallel", "arbitrary"), vmem_limit_bytes=VMEM_LIMIT),
        name="mix_backward_final" if final else "mix_backward",
    )(*args)


def _permute_in_columns(w):
    d = w.shape[0]
    sc0 = 2592
    at0 = sc0 + 4 * SC_WIDTH
    parts = [
        w[:, 1024:2048],
        w[:, 2048:2560],
        w[:, sc0:sc0 + 1024],
        w[:, 0:1024],
        w[:, sc0 + 1024:at0],
        w[:, at0:at0 + 512],
        w[:, at0 + 768:at0 + 1280],
        w[:, at0 + 512:at0 + 768],
        w[:, 2560:2592],
        jnp.zeros((d, PW - W_DT - 2 * SSD_HEADS), w.dtype),
    ]
    return jnp.concatenate(parts, axis=1)


def _rope_table(n_ctx, n_lat):
    n_rows = n_lat // GRID_W
    rows = np.repeat(np.arange(n_rows), GRID_W).astype(np.float32)
    cols = np.tile(np.arange(GRID_W), n_rows).astype(np.float32)
    axis_dim = HEAD_DIM // 2
    inv_freq = jnp.asarray(ROPE_BASE, F32) ** (-jnp.arange(0, axis_dim, 2, dtype=F32) / axis_dim)
    ang = jnp.concatenate([jnp.asarray(rows)[:, None] * inv_freq,
                           jnp.asarray(cols)[:, None] * inv_freq], axis=-1)
    cos, sin = jnp.cos(ang), jnp.sin(ang)
    q = HEAD_DIM // 4
    cos_h = jnp.concatenate([cos[:, 0:q], cos[:, 0:q], cos[:, q:2 * q], cos[:, q:2 * q]], axis=1)
    sin_h = jnp.concatenate([-sin[:, 0:q], sin[:, 0:q], -sin[:, q:2 * q], sin[:, q:2 * q]], axis=1)
    tab = jnp.concatenate([cos_h, cos_h, sin_h, sin_h], axis=1)
    ident = jnp.concatenate([jnp.ones((n_ctx, LANES), F32), jnp.zeros((n_ctx, LANES), F32)], axis=1)
    return jnp.concatenate([ident, tab], axis=0)


def _expand_matrix():
    e = np.zeros((2, LANES, SSD_WIDTH), np.float32)
    for d in range(2):
        for h in range(SSD_HEADS):
            e[d, d * SSD_HEADS + h, h * HEAD_DIM:(h + 1) * HEAD_DIM] = 1.0
    return jnp.asarray(e, BF16)


def _band_bias(n_ctx):
    row = np.arange(2 * CHUNK)[:, None] % CHUNK
    col = np.arange(3 * CHUNK + n_ctx)[None, :]
    ok = (col >= 3 * CHUNK) | ((col >= row) & (col <= row + 2 * CHUNK))
    return jnp.asarray(np.where(ok, 0.0, NEG_INF), F32)


def _dir_rows(v):
    out = jnp.zeros((2, 1, LANES), F32)
    for d in range(2):
        out = out.at[d, 0, d * SSD_HEADS:(d + 1) * SSD_HEADS].set(v[d])
    return out


def kernel(x, c, ctx, c_ctx, norm_w, w_mod, b_mod, w_in, ssd_conv_w, ssd_conv_b, ssd_dt_bias,
           ssd_a_log, ssd_d, ssd_norm_w, sc_conv_w, attn_sink, w_out, final_norm_w):
    b, n_lat, d = x.shape
    n_ctx = ctx.shape[1]
    depth = w_in.shape[0]
    assert d == D_MODEL and n_ctx % ROW_TILE == 0 and n_lat % ROW_TILE == 0 and b <= 7
    n_ctx_chunks = n_ctx // CHUNK
    n_ctx_tiles = n_ctx // ROW_TILE

    cvec = jnp.zeros((8, d), F32).at[0:b].set(c).at[b].set(c_ctx)
    mod = _modulation(cvec, w_mod, b_mod)
    mod = mod.reshape(depth, 8, 3, d)
    modsel = jnp.stack([jnp.broadcast_to(mod[:, b][:, None], (depth, b, 3, d)), mod[:, 0:b]], axis=2)

    xa = jnp.concatenate([ctx, x], axis=1)
    rope_tab = _rope_table(n_ctx, n_lat)
    expand = _expand_matrix()
    band_bias = _band_bias(n_ctx)

    out = None
    for l in range(depth):
        last = l == depth - 1
        w_in_p = _permute_in_columns(w_in[l].astype(BF16))
        cw = ssd_conv_w[l]
        cbias = ssd_conv_b[l]
        consts = {
            "cw_x": cw[:, 0:SSD_WIDTH], "cb_x": cbias[0:SSD_WIDTH].reshape(1, -1),
            "cw_bc": cw[:, SSD_WIDTH:], "cb_bc": cbias[SSD_WIDTH:].reshape(1, -1),
            "cw_sc": sc_conv_w[l],
            "dt_bias": _dir_rows(ssd_dt_bias[l]), "a_log": _dir_rows(ssd_a_log[l]),
            "expand": expand,
            "d_skip": jnp.repeat(ssd_d[l], HEAD_DIM).reshape(1, -1),
            "ssd_norm_w": ssd_norm_w[l].reshape(1, -1),
            "sink": jnp.broadcast_to(attn_sink[l][:, None], (ATTN_HEADS, LANES)),
            "band_bias": band_bias,
        }
        pf, pb = _inproj(xa, modsel[l], norm_w[l], w_in_p, consts, rope_tab, n_ctx_tiles)
        yf = _ssd_forward(pf, pb, consts)
        w_out_b = w_out[l].astype(BF16)
        if last:
            out = _mix_backward(pf, pb, yf, xa, modsel[l], w_out_b, consts, n_ctx_chunks, final_norm_w)
        else:
            xa = _mix_backward(pf, pb, yf, xa, modsel[l], w_out_b, consts, n_ctx_chunks)
    return out
```

```python
import functools

import numpy as np
import jax
import jax.numpy as jnp
from jax import lax
from jax.experimental import pallas as pl
from jax.experimental.pallas import tpu as pltpu

F32 = jnp.float32
BF16 = jnp.bfloat16

EPS = 1e-6
NEG_INF = -1e30
ROPE_BASE = 10000.0
GRID_W = 64

D_MODEL = 1024
CHUNK = 128
HALO = 8
ROW_TILE = 256
SSD_WIDTH = 1024
SSD_HEADS = 16
SSD_GROUPS = 2
SSD_STATE = 128
GROUP_W = SSD_WIDTH // SSD_GROUPS
BC_WIDTH = 2 * SSD_GROUPS * SSD_STATE
SC_WIDTH = 512
ATTN_WIDTH = 512
ATTN_HEADS = 8
ATTN_REP = 4
HEAD_DIM = 64
MIX_WIDTH = 2048
LANES = 128

W_X = 0
W_BC = 1024
W_VC = 1536
W_ZS = 2560
W_BZ = 3584
W_Q = 4608
W_ZA = 5120
W_KV = 5632
W_DT = 5888
PW = 6016

F_XC = 0
F_ZS = 1024
F_GA = 2048
F_DT = 2560
FW = 2688
B_Q = 0
B_BC = 1024
B_YC = 1536
B_KV = 2048
BW = 2304
SCAN_ROWS = 2 * CHUNK

VMEM_LIMIT = 56 * 1024 * 1024


def _sigmoid(v):
    return 1.0 / (1.0 + jnp.exp(-v))


def _silu(v):
    return v * _sigmoid(v)


def _softplus(v):
    return jnp.maximum(v, 0.0) + jnp.log1p(jnp.exp(-jnp.abs(v)))


def _split2(v):
    hi = v.astype(BF16)
    lo = (v - hi.astype(F32)).astype(BF16)
    return hi, lo


def _split3(v):
    hi = v.astype(BF16)
    r = v - hi.astype(F32)
    mid = r.astype(BF16)
    lo = (r - mid.astype(F32)).astype(BF16)
    return hi, mid, lo


def _dot(a, b):
    return jnp.dot(a, b, preferred_element_type=F32)


def _dot_nt(a, b):
    return lax.dot_general(a, b, (((1,), (1,)), ((), ())), preferred_element_type=F32)


def _rmsnorm(x, w):
    ms = jnp.mean(x * x, axis=-1, keepdims=True)
    return x * lax.rsqrt(ms + EPS) * w


def _mod_kernel(c_ref, w_ref, b_ref, o_ref):
    sc = _silu(c_ref[...])
    o_ref[...] = _dot(sc.astype(BF16), w_ref[...].astype(BF16)) + b_ref[...]


def _modulation(cvec, w_mod, b_mod):
    depth = w_mod.shape[0]
    nblk = w_mod.shape[2] // D_MODEL
    return pl.pallas_call(
        _mod_kernel,
        grid=(depth, nblk),
        in_specs=[
            pl.BlockSpec((8, D_MODEL), lambda l, j: (0, 0)),
            pl.BlockSpec((None, D_MODEL, D_MODEL), lambda l, j: (l, 0, j)),
            pl.BlockSpec((None, 1, D_MODEL), lambda l, j: (l, 0, j)),
        ],
        out_specs=pl.BlockSpec((None, 8, D_MODEL), lambda l, j: (l, 0, j)),
        out_shape=jax.ShapeDtypeStruct((depth, 8, w_mod.shape[2]), F32),
        compiler_params=pltpu.CompilerParams(vmem_limit_bytes=VMEM_LIMIT),
        name="modulation",
    )(cvec, w_mod, b_mod.reshape(depth, 1, -1))


def _conv3(t, prev_row, next_row, w_ref):
    n = t.shape[0]
    rows = lax.broadcasted_iota(jnp.int32, t.shape, 0)
    tp = jnp.where(rows == 0, prev_row, pltpu.roll(t, 1, 0))
    tn = jnp.where(rows == n - 1, next_row, pltpu.roll(t, n - 1, 0))
    return w_ref[0:1, :] * tp + w_ref[1:2, :] * t + w_ref[2:3, :] * tn


def _rope(t, cos, sin_signed):
    lane = lax.broadcasted_iota(jnp.int32, t.shape, 1)
    first = (lane % 32) < 16
    swapped = jnp.where(first, pltpu.roll(t, LANES - 16, 1), pltpu.roll(t, 16, 1))
    return t * cos + swapped * sin_signed


def _inproj_kernel(split, n_ctx_tiles, n_tiles, *refs):
    if split:
        ctx_ref, refs = refs[0], refs[1:]
    (x_ref, xp_ref, xn_ref, mod_ref, nw_ref, w_ref, cwx_ref, cbx_ref, cwbc_ref,
     cbbc_ref, cwsc_ref, rope_ref, f_ref, b_ref) = refs
    i = pl.program_id(1)
    has_prev = jnp.logical_and(i != 0, i != n_ctx_tiles).astype(F32)
    has_next = jnp.logical_and(i != n_ctx_tiles - 1, i != n_tiles - 1).astype(F32)

    def norm_mod(x):
        return (_rmsnorm(x, nw_ref[...]) * (1.0 + mod_ref[1:2, :]) + mod_ref[0:1, :]).astype(BF16)

    x_tile = x_ref[...]
    if split:
        x_tile = jnp.where(i < n_ctx_tiles, ctx_ref[...], x_tile)
    h = norm_mod(x_tile)
    hh = jnp.concatenate([norm_mod(xp_ref[...]), norm_mod(xn_ref[...])], axis=0)

    def proj(col, width):
        return _dot(h, w_ref[:, col:col + width])

    def proj_halo(col, width):
        ph = _dot(hh, w_ref[:, col:col + width])
        return ph[HALO - 1:HALO, :] * has_prev, ph[HALO:HALO + 1, :] * has_next

    half = SSD_WIDTH // 2
    for k in range(2):
        pv, nx = proj_halo(W_X + k * half, half)
        y = _conv3(proj(W_X + k * half, half), pv, nx, cwx_ref.at[:, k * half:(k + 1) * half])
        f_ref[:, F_XC + k * half:F_XC + (k + 1) * half] = _silu(y + cbx_ref[:, k * half:(k + 1) * half])
    pv, nx = proj_halo(W_BC, BC_WIDTH)
    y = _conv3(proj(W_BC, BC_WIDTH), pv, nx, cwbc_ref) + cbbc_ref[...]
    b_ref[:, B_BC:B_BC + BC_WIDTH] = _silu(y).astype(BF16)

    w = SC_WIDTH
    vc = proj(W_VC, 2 * w)
    vch = _dot(hh, w_ref[:, W_VC:W_VC + 2 * w])
    cvh = vch[:, 0:w] * vch[:, w:2 * w]
    conv = _conv3(vc[:, 0:w] * vc[:, w:2 * w], cvh[HALO - 1:HALO, :] * has_prev,
                  cvh[HALO:HALO + 1, :] * has_next, cwsc_ref)
    bz = proj(W_BZ, 2 * w)
    b_ref[:, B_YC:B_YC + w] = (bz[:, 0:w] * conv * _silu(bz[:, w:2 * w])).astype(BF16)

    for k in range(2):
        f_ref[:, F_ZS + k * half:F_ZS + (k + 1) * half] = _silu(proj(W_ZS + k * half, half))
    f_ref[:, F_GA:F_GA + ATTN_WIDTH] = _silu(proj(W_ZA, ATTN_WIDTH))
    f_ref[:, F_DT:F_DT + LANES] = proj(W_DT, LANES)

    cos, sin = rope_ref[:, 0:LANES], rope_ref[:, LANES:2 * LANES]
    left = lax.broadcasted_iota(jnp.int32, (ROW_TILE, LANES), 1) < HEAD_DIM
    q = proj(W_Q, ATTN_WIDTH)
    scale = HEAD_DIM ** -0.5
    for j in range(ATTN_WIDTH // LANES):
        qs = _rope(q[:, j * LANES:(j + 1) * LANES], cos, sin) * scale
        qr = pltpu.roll(qs, HEAD_DIM, 1)
        if j // 2 == 0:
            h0, h1 = jnp.where(left, qs, 0.0), jnp.where(left, qr, 0.0)
        else:
            h0, h1 = jnp.where(left, 0.0, qr), jnp.where(left, 0.0, qs)
        b_ref[:, B_Q + 2 * j * LANES:B_Q + (2 * j + 1) * LANES] = h0.astype(BF16)
        b_ref[:, B_Q + (2 * j + 1) * LANES:B_Q + (2 * j + 2) * LANES] = h1.astype(BF16)
    kv = proj(W_KV, 2 * LANES)
    b_ref[:, B_KV:B_KV + LANES] = _rope(kv[:, 0:LANES], cos, sin).astype(BF16)
    b_ref[:, B_KV + LANES:B_KV + 2 * LANES] = kv[:, LANES:2 * LANES].astype(BF16)


def _inproj(xa, ctx, modsel, norm_w, w_in_p, consts, rope_tab, n_ctx_tiles):
    split = ctx is not None
    b, rows, d = xa.shape
    off = n_ctx_tiles if split else 0
    t = rows + off * ROW_TILE
    n_tiles = t // ROW_TILE
    per = ROW_TILE // HALO
    n_halo = rows // HALO

    def const(shape):
        return pl.BlockSpec(shape, lambda bi, i: (0,) * len(shape))

    in_specs = [
            pl.BlockSpec((None, ROW_TILE, d), lambda bi, i: (bi, jnp.maximum(i - off, 0), 0)),
            pl.BlockSpec((None, HALO, d),
                         lambda bi, i: (bi, jnp.maximum((i - off) * per - 1, 0), 0)),
            pl.BlockSpec((None, HALO, d),
                         lambda bi, i: (bi, jnp.clip((i - off + 1) * per, 0, n_halo - 1), 0)),
            pl.BlockSpec((None, None, 3, d),
                         lambda bi, i: (bi, jnp.where(i >= n_ctx_tiles, 1, 0), 0, 0)),
            const((1, d)),
            pl.BlockSpec((d, PW), lambda bi, i: (0, 0), pipeline_mode=pl.Buffered(1)),
            const((3, SSD_WIDTH)), const((1, SSD_WIDTH)), const((3, BC_WIDTH)), const((1, BC_WIDTH)),
            const((3, SC_WIDTH)),
            pl.BlockSpec((ROW_TILE, 2 * LANES), lambda bi, i: (i, 0)),
    ]
    args = [xa, xa, xa, modsel, norm_w.reshape(1, d), w_in_p, consts["cw_x"], consts["cb_x"],
            consts["cw_bc"], consts["cb_bc"], consts["cw_sc"], rope_tab]
    if split:
        assert ctx.shape[1] == ROW_TILE and n_ctx_tiles == 1
        in_specs.insert(0, pl.BlockSpec((None, ROW_TILE, d), lambda bi, i: (bi, 0, 0)))
        args.insert(0, ctx)
    return pl.pallas_call(
        functools.partial(_inproj_kernel, split, n_ctx_tiles, n_tiles),
        grid=(b, n_tiles),
        in_specs=in_specs,
        out_specs=[
            pl.BlockSpec((None, ROW_TILE, FW), lambda bi, i: (bi, i, 0)),
            pl.BlockSpec((None, ROW_TILE, BW), lambda bi, i: (bi, i, 0)),
        ],
        out_shape=[jax.ShapeDtypeStruct((b, t, FW), F32), jax.ShapeDtypeStruct((b, t, BW), BF16)],
        compiler_params=pltpu.CompilerParams(
            dimension_semantics=("parallel", "parallel"), vmem_limit_bytes=VMEM_LIMIT),
        name="in_projection",
    )(*args)


def _scan_chunk(step, direction, n_ctx_chunks, n_chunks):
    if direction == 0:
        return step
    return jnp.where(step < n_ctx_chunks, n_ctx_chunks - 1 - step,
                     n_chunks - 1 - (step - n_ctx_chunks))


def _ssd_prepare(direction, xc, dt_raw, dtb_ref, alog_ref, e_ref):
    lane = lax.broadcasted_iota(jnp.int32, (1, LANES), 1)
    mine = jnp.logical_and(lane >= SSD_HEADS * direction, lane < SSD_HEADS * (direction + 1))
    a_row = jnp.where(mine, -jnp.exp(alog_ref[...]), 0.0)
    dt = _softplus(dt_raw + dtb_ref[...])
    da = dt * a_row

    r_i = lax.broadcasted_iota(jnp.int32, (CHUNK, CHUNK), 0)
    c_i = lax.broadcasted_iota(jnp.int32, (CHUNK, CHUNK), 1)
    tri = (c_i <= r_i) if direction == 0 else (c_i >= r_i)
    tri_b = jnp.where(tri, 1.0, 0.0).astype(BF16)
    u3 = _dot(tri_b, jnp.concatenate(_split3(da), axis=1))
    u = u3[:, 0:LANES] + u3[:, LANES:2 * LANES] + u3[:, 2 * LANES:3 * LANES]
    tot_row = CHUNK - 1 if direction == 0 else 0
    tot = u[tot_row:tot_row + 1, :]
    exp_u = jnp.exp(u)
    exp_d = jnp.exp(tot - u)

    stack = jnp.concatenate((dt.astype(BF16), exp_d.astype(BF16)) + _split2(exp_u), axis=0)
    ex = _dot(stack, e_ref[...])
    xdt = xc * ex[0:CHUNK]
    xdec = xdt * ex[CHUNK:2 * CHUNK]
    eu_e = ex[2 * CHUNK:3 * CHUNK] + ex[3 * CHUNK:4 * CHUNK]
    return dict(direction=direction, tri=tri, u=u, u_t=u.T, xdt=xdt, xdec=xdec, eu_e=eu_e,
                tot_row=tot_row)


def _ssd_output(prep, bc, state_ref):
    direction, tri, u, u_t = prep["direction"], prep["tri"], prep["u"], prep["u_t"]
    xdt, eu_e = prep["xdt"], prep["eu_e"]
    half = lax.broadcasted_iota(jnp.int32, (CHUNK, LANES), 1) < HEAD_DIM
    y_slabs = []
    for g in range(SSD_GROUPS):
        b_g = bc[:, g * SSD_STATE:(g + 1) * SSD_STATE]
        c_g = bc[:, (SSD_GROUPS + g) * SSD_STATE:(SSD_GROUPS + g + 1) * SSD_STATE]
        cb = _dot_nt(c_g, b_g)
        y_off = _dot(c_g, state_ref[g].astype(BF16)) * eu_e[:, g * GROUP_W:(g + 1) * GROUP_W]
        for j in range(GROUP_W // LANES):
            ms = []
            for hh in range(2):
                hl = SSD_HEADS * direction + g * (SSD_HEADS // SSD_GROUPS) + 2 * j + hh
                diff = u[:, hl:hl + 1] - u_t[hl:hl + 1, :]
                ms.append(cb * jnp.exp(jnp.where(tri, diff, -jnp.inf)))
            m2 = jnp.concatenate(ms, axis=1).astype(BF16)
            lo = g * GROUP_W + j * LANES
            slab = xdt[:, lo:lo + LANES]
            xbd = jnp.concatenate([jnp.where(half, slab, 0.0), jnp.where(half, 0.0, slab)],
                                  axis=0).astype(BF16)
            y_slabs.append(_dot(m2, xbd) + y_off[:, j * LANES:(j + 1) * LANES])
    return jnp.concatenate(y_slabs, axis=1)


def _ssd_update_state(prep, bc, state_ref):
    eu_e, xdec, tot_row = prep["eu_e"], prep["xdec"], prep["tot_row"]
    for g in range(SSD_GROUPS):
        b_g = bc[:, g * SSD_STATE:(g + 1) * SSD_STATE]
        decay = eu_e[tot_row:tot_row + 1, g * GROUP_W:(g + 1) * GROUP_W]
        state_ref[g] = state_ref[g] * decay + _dot(
            b_g.astype(F32).T.astype(BF16), xdec[:, g * GROUP_W:(g + 1) * GROUP_W].astype(BF16))


def _ssd_fwd_kernel(xc_ref, bc_ref, dt_ref, dtb_ref, alog_ref, e_ref, o_ref, state_ref):
    @pl.when(pl.program_id(1) == 0)
    def _():
        state_ref[...] = jnp.zeros_like(state_ref)

    for k in range(SCAN_ROWS // CHUNK):
        rows = slice(k * CHUNK, (k + 1) * CHUNK)
        prep = _ssd_prepare(0, xc_ref[rows, :], dt_ref[rows, :], dtb_ref, alog_ref, e_ref)
        bc = bc_ref[rows, :]
        o_ref[rows, :] = _ssd_output(prep, bc, state_ref)
        _ssd_update_state(prep, bc, state_ref)


def _ssd_forward(pf, pb, consts):
    b, t, _ = pf.shape
    return pl.pallas_call(
        _ssd_fwd_kernel,
        grid=(b, t // SCAN_ROWS),
        in_specs=[
            pl.BlockSpec((None, SCAN_ROWS, SSD_WIDTH), lambda bi, s: (bi, s, F_XC // SSD_WIDTH)),
            pl.BlockSpec((None, SCAN_ROWS, BC_WIDTH), lambda bi, s: (bi, s, B_BC // BC_WIDTH)),
            pl.BlockSpec((None, SCAN_ROWS, LANES), lambda bi, s: (bi, s, F_DT // LANES)),
            pl.BlockSpec((None, 1, LANES), lambda bi, s: (0, 0, 0)),
            pl.BlockSpec((None, 1, LANES), lambda bi, s: (0, 0, 0)),
            pl.BlockSpec((None, LANES, SSD_WIDTH), lambda bi, s: (0, 0, 0)),
        ],
        out_specs=pl.BlockSpec((None, SCAN_ROWS, SSD_WIDTH), lambda bi, s: (bi, s, 0)),
        out_shape=jax.ShapeDtypeStruct((b, t, SSD_WIDTH), F32),
        scratch_shapes=[pltpu.VMEM((SSD_GROUPS, SSD_STATE, GROUP_W), F32)],
        compiler_params=pltpu.CompilerParams(
            dimension_semantics=("parallel", "arbitrary"), vmem_limit_bytes=VMEM_LIMIT),
        name="ssd_forward",
    )(pf, pb, pf, consts["dt_bias"], consts["a_log"], consts["expand"])


def _attend(q_ref, rows, k_all, v_all, bias, sink_ref):
    left = lax.broadcasted_iota(jnp.int32, (CHUNK, LANES), 1) < HEAD_DIM
    outs = []
    for j in range(ATTN_HEADS // 2):
        q2 = jnp.concatenate([q_ref[rows, 2 * j * LANES:(2 * j + 1) * LANES],
                              q_ref[rows, (2 * j + 1) * LANES:(2 * j + 2) * LANES]], axis=0)
        s = _dot_nt(q2, k_all)
        if bias is not None:
            s = s + bias
        sk = jnp.concatenate([
            jnp.broadcast_to(sink_ref[2 * j:2 * j + 1, 0:1], (CHUNK, 1)),
            jnp.broadcast_to(sink_ref[2 * j + 1:2 * j + 2, 0:1], (CHUNK, 1))], axis=0)
        m = jnp.maximum(jnp.max(s, axis=-1, keepdims=True), sk)
        p = jnp.exp(s - m)
        denom = jnp.sum(p, axis=-1, keepdims=True) + jnp.exp(sk - m)
        o = _dot(p.astype(BF16), v_all) / denom
        o0, o1 = o[0:CHUNK], o[CHUNK:2 * CHUNK]
        if j // 2 == 0:
            outs.append(jnp.where(left, o0, pltpu.roll(o1, HEAD_DIM, 1)))
        else:
            outs.append(jnp.where(left, pltpu.roll(o0, HEAD_DIM, 1), o1))
    return outs


def _mix_bwd_kernel(final, split, n_ctx_blocks, n_blocks, *refs):
    if split:
        ctx_ref, refs = refs[0], refs[1:]
    (xc_ref, bc_ref, dt_ref, zs_ref, yf_ref, yc_ref, ga_ref, q_ref, kv_ref, kvp_ref, kvn_ref,
     kvc_ref, x_ref, mod_ref, w_ref, dtb_ref, alog_ref, e_ref, dsk_ref, nw_ref, sink_ref,
     bias_ref) = refs[:22]
    if final:
        fw_ref, o_ref, state_ref, ycat_ref = refs[22:]
    else:
        o_ref, state_ref, ycat_ref = refs[22:]

    step = pl.program_id(1)
    blk = _scan_chunk(step, 1, n_ctx_blocks, n_blocks)
    per = SCAN_ROWS // CHUNK
    n_ctx_chunks, n_chunks = per * n_ctx_blocks, per * n_blocks

    @pl.when(step == 0)
    def _():
        state_ref[...] = jnp.zeros_like(state_ref)

    def guarded(fn):
        if final:
            pl.when(blk >= n_ctx_blocks)(fn)
        else:
            fn()

    for k in reversed(range(per)):
        rows = slice(k * CHUNK, (k + 1) * CHUNK)
        c = blk * per + k
        xc = xc_ref[rows, :]
        bc = bc_ref[rows, :]
        prep = _ssd_prepare(1, xc, dt_ref[rows, :], dtb_ref, alog_ref, e_ref)

        def emit(k=k, rows=rows, c=c, xc=xc, bc=bc, prep=prep):
            y = _ssd_output(prep, bc, state_ref) + yf_ref[rows, :] + xc * dsk_ref[...]
            gated = y * zs_ref[rows, :]
            for g in range(SSD_GROUPS):
                gg = gated[:, g * GROUP_W:(g + 1) * GROUP_W]
                ms = jnp.mean(gg * gg, axis=-1, keepdims=True)
                ycat_ref[rows, g * GROUP_W:(g + 1) * GROUP_W] = (
                    gg * lax.rsqrt(ms + EPS) * nw_ref[:, g * GROUP_W:(g + 1) * GROUP_W]).astype(BF16)

            def put_attention(slabs):
                ycat_ref[rows, SSD_WIDTH + SC_WIDTH:MIX_WIDTH] = (
                    jnp.concatenate(slabs, axis=1) * ga_ref[rows, :]).astype(BF16)

            if not final:
                @pl.when(blk < n_ctx_blocks)
                def _():
                    put_attention(_attend(q_ref, rows, kvc_ref[:, 0:LANES], kvc_ref[:, LANES:2 * LANES],
                                          None, sink_ref))

            @pl.when(blk >= n_ctx_blocks)
            def _():
                halves = [kvp_ref] + [kv_ref.at[i * CHUNK:(i + 1) * CHUNK, :] for i in range(per)] + [kvn_ref]
                win = halves[k:k + 3] + [kvc_ref]
                k_all = jnp.concatenate([r[:, 0:LANES] for r in win], axis=0)
                v_all = jnp.concatenate([r[:, LANES:2 * LANES] for r in win], axis=0)
                col = lax.broadcasted_iota(jnp.int32, (1, bias_ref.shape[1]), 1)
                no_prev = jnp.where(c > n_ctx_chunks, 0.0, NEG_INF)
                no_next = jnp.where(c < n_chunks - 1, 0.0, NEG_INF)
                edge = jnp.where(col < CHUNK, no_prev,
                                 jnp.where(jnp.logical_and(col >= 2 * CHUNK, col < 3 * CHUNK), no_next, 0.0))
                put_attention(_attend(q_ref, rows, k_all, v_all, bias_ref[...] + edge, sink_ref))

        guarded(emit)
        _ssd_update_state(prep, bc, state_ref)

    def project():
        acc = _dot(ycat_ref[:, 0:SSD_WIDTH], w_ref[0:SSD_WIDTH, :])
        acc = acc + _dot(yc_ref[...], w_ref[SSD_WIDTH:SSD_WIDTH + SC_WIDTH, :])
        acc = acc + _dot(ycat_ref[:, SSD_WIDTH + SC_WIDTH:MIX_WIDTH], w_ref[SSD_WIDTH + SC_WIDTH:MIX_WIDTH, :])
        x_in = x_ref[...]
        if split:
            x_in = jnp.where(blk < n_ctx_blocks, ctx_ref[...], x_in)
        xn = x_in + mod_ref[2:3, :] * acc
        if final:
            xn = _rmsnorm(xn, fw_ref[...])
        o_ref[...] = xn

    guarded(project)


def _mix_backward(pf, pb, yf, xa, ctx, modsel, w_out_b, consts, n_ctx_blocks, final_norm_w=None):
    split = ctx is not None
    b, t, _ = pf.shape
    d = xa.shape[-1]
    n_blocks = t // SCAN_ROWS
    per = SCAN_ROWS // CHUNK
    n_chunks = per * n_blocks
    final = final_norm_w is not None
    n_ctx = n_ctx_blocks * SCAN_ROWS
    x_off = n_ctx_blocks if split else 0

    def bidx(s):
        return _scan_chunk(s, 1, n_ctx_blocks, n_blocks)

    def blk(width, offset):
        return pl.BlockSpec((None, SCAN_ROWS, width), lambda bi, s: (bi, bidx(s), offset // width))

    def const(shape):
        return pl.BlockSpec(shape, lambda bi, s: (0,) * len(shape))

    kvw = 2 * LANES
    in_specs = [
        blk(SSD_WIDTH, F_XC), blk(BC_WIDTH, B_BC), blk(LANES, F_DT), blk(SSD_WIDTH, F_ZS),
        blk(SSD_WIDTH, 0),
        blk(SC_WIDTH, B_YC), blk(ATTN_WIDTH, F_GA), blk(2 * ATTN_WIDTH, B_Q),
        blk(kvw, B_KV),
        pl.BlockSpec((None, CHUNK, kvw),
                     lambda bi, s: (bi, jnp.maximum(bidx(s) * per - 1, 0), B_KV // kvw)),
        pl.BlockSpec((None, CHUNK, kvw),
                     lambda bi, s: (bi, jnp.minimum((bidx(s) + 1) * per, n_chunks - 1), B_KV // kvw)),
        pl.BlockSpec((None, n_ctx, kvw), lambda bi, s: (bi, 0, B_KV // kvw)),
        pl.BlockSpec((None, SCAN_ROWS, d),
                     lambda bi, s: (bi, jnp.maximum(bidx(s) - x_off, 0), 0)),
        pl.BlockSpec((None, None, 3, d),
                     lambda bi, s: (bi, jnp.where(bidx(s) >= n_ctx_blocks, 1, 0), 0, 0)),
        pl.BlockSpec((MIX_WIDTH, d), lambda bi, s: (0, 0), pipeline_mode=pl.Buffered(1)),
        pl.BlockSpec((None, 1, LANES), lambda bi, s: (1, 0, 0)),
        pl.BlockSpec((None, 1, LANES), lambda bi, s: (1, 0, 0)),
        pl.BlockSpec((None, LANES, SSD_WIDTH), lambda bi, s: (1, 0, 0)),
        const((1, SSD_WIDTH)), const((1, SSD_WIDTH)), const((ATTN_HEADS, LANES)),
        const((2 * CHUNK, 3 * CHUNK + n_ctx)),
    ]
    args = [pf, pb, pf, pf, yf, pb, pf, pb, pb, pb, pb, pb, xa, modsel, w_out_b,
            consts["dt_bias"], consts["a_log"], consts["expand"], consts["d_skip"],
            consts["ssd_norm_w"], consts["sink"], consts["band_bias"]]
    if split:
        assert ctx.shape[1] == SCAN_ROWS and n_ctx_blocks == 1
        in_specs.insert(0, pl.BlockSpec((None, SCAN_ROWS, d), lambda bi, s: (bi, 0, 0)))
        args.insert(0, ctx)
    if final:
        in_specs.append(const((1, d)))
        args.append(final_norm_w.reshape(1, d))
        out_spec = pl.BlockSpec(
            (None, SCAN_ROWS, d),
            lambda bi, s: (bi, jnp.where(bidx(s) >= n_ctx_blocks, bidx(s), n_blocks - 1) - n_ctx_blocks, 0))
        out_shape = jax.ShapeDtypeStruct((b, t - n_ctx, d), F32)
    else:
        out_spec = blk(d, 0)
        out_shape = jax.ShapeDtypeStruct((b, t, d), F32)
    return pl.pallas_call(
        functools.partial(_mix_bwd_kernel, final, split, n_ctx_blocks, n_blocks),
        grid=(b, n_blocks),
        in_specs=in_specs,
        out_specs=out_spec,
        out_shape=out_shape,
        scratch_shapes=[pltpu.VMEM((SSD_GROUPS, SSD_STATE, GROUP_W), F32),
                        pltpu.VMEM((SCAN_ROWS, MIX_WIDTH), BF16)],
        compiler_params=pltpu.CompilerParams(
            dimension_semantics=("parallel", "arbitrary"), vmem_limit_bytes=VMEM_LIMIT),
        name="mix_backward_final" if final else "mix_backward",
    )(*args)


def _permute_in_columns(w):
    d = w.shape[0]
    sc0 = 2592
    at0 = sc0 + 4 * SC_WIDTH
    parts = [
        w[:, 1024:2048],
        w[:, 2048:2560],
        w[:, sc0:sc0 + 1024],
        w[:, 0:1024],
        w[:, sc0 + 1024:at0],
        w[:, at0:at0 + 512],
        w[:, at0 + 768:at0 + 1280],
        w[:, at0 + 512:at0 + 768],
        w[:, 2560:2592],
        jnp.zeros((d, PW - W_DT - 2 * SSD_HEADS), w.dtype),
    ]
    return jnp.concatenate(parts, axis=1)


def _rope_table(n_ctx, n_lat):
    n_rows = n_lat // GRID_W
    rows = np.repeat(np.arange(n_rows), GRID_W).astype(np.float32)
    cols = np.tile(np.arange(GRID_W), n_rows).astype(np.float32)
    axis_dim = HEAD_DIM // 2
    inv_freq = jnp.asarray(ROPE_BASE, F32) ** (-jnp.arange(0, axis_dim, 2, dtype=F32) / axis_dim)
    ang = jnp.concatenate([jnp.asarray(rows)[:, None] * inv_freq,
                           jnp.asarray(cols)[:, None] * inv_freq], axis=-1)
    cos, sin = jnp.cos(ang), jnp.sin(ang)
    q = HEAD_DIM // 4
    cos_h = jnp.concatenate([cos[:, 0:q], cos[:, 0:q], cos[:, q:2 * q], cos[:, q:2 * q]], axis=1)
    sin_h = jnp.concatenate([-sin[:, 0:q], sin[:, 0:q], -sin[:, q:2 * q], sin[:, q:2 * q]], axis=1)
    tab = jnp.concatenate([cos_h, cos_h, sin_h, sin_h], axis=1)
    ident = jnp.concatenate([jnp.ones((n_ctx, LANES), F32), jnp.zeros((n_ctx, LANES), F32)], axis=1)
    return jnp.concatenate([ident, tab], axis=0)


def _expand_matrix():
    e = np.zeros((2, LANES, SSD_WIDTH), np.float32)
    for d in range(2):
        for h in range(SSD_HEADS):
            e[d, d * SSD_HEADS + h, h * HEAD_DIM:(h + 1) * HEAD_DIM] = 1.0
    return jnp.asarray(e, BF16)


def _band_bias(n_ctx):
    row = np.arange(2 * CHUNK)[:, None] % CHUNK
    col = np.arange(3 * CHUNK + n_ctx)[None, :]
    ok = (col >= 3 * CHUNK) | ((col >= row) & (col <= row + 2 * CHUNK))
    return jnp.asarray(np.where(ok, 0.0, NEG_INF), F32)


def _dir_rows(v):
    out = jnp.zeros((2, 1, LANES), F32)
    for d in range(2):
        out = out.at[d, 0, d * SSD_HEADS:(d + 1) * SSD_HEADS].set(v[d])
    return out


def kernel(x, c, ctx, c_ctx, norm_w, w_mod, b_mod, w_in, ssd_conv_w, ssd_conv_b, ssd_dt_bias,
           ssd_a_log, ssd_d, ssd_norm_w, sc_conv_w, attn_sink, w_out, final_norm_w):
    b, n_lat, d = x.shape
    n_ctx = ctx.shape[1]
    depth = w_in.shape[0]
    assert d == D_MODEL and n_ctx % ROW_TILE == 0 and n_lat % ROW_TILE == 0 and b <= 7
    n_ctx_chunks = n_ctx // CHUNK
    n_ctx_tiles = n_ctx // ROW_TILE

    cvec = jnp.zeros((8, d), F32).at[0:b].set(c).at[b].set(c_ctx)
    mod = _modulation(cvec, w_mod, b_mod)
    mod = mod.reshape(depth, 8, 3, d)
    modsel = jnp.stack([jnp.broadcast_to(mod[:, b][:, None], (depth, b, 3, d)), mod[:, 0:b]], axis=2)

    xa, ctx_in = x, ctx
    assert n_ctx == ROW_TILE == SCAN_ROWS
    n_ctx_blocks = n_ctx // SCAN_ROWS
    rope_tab = _rope_table(n_ctx, n_lat)
    expand = _expand_matrix()
    band_bias = _band_bias(n_ctx)

    out = None
    for l in range(depth):
        last = l == depth - 1
        w_in_p = _permute_in_columns(w_in[l].astype(BF16))
        cw = ssd_conv_w[l]
        cbias = ssd_conv_b[l]
        consts = {
            "cw_x": cw[:, 0:SSD_WIDTH], "cb_x": cbias[0:SSD_WIDTH].reshape(1, -1),
            "cw_bc": cw[:, SSD_WIDTH:], "cb_bc": cbias[SSD_WIDTH:].reshape(1, -1),
            "cw_sc": sc_conv_w[l],
            "dt_bias": _dir_rows(ssd_dt_bias[l]), "a_log": _dir_rows(ssd_a_log[l]),
            "expand": expand,
            "d_skip": jnp.repeat(ssd_d[l], HEAD_DIM).reshape(1, -1),
            "ssd_norm_w": ssd_norm_w[l].reshape(1, -1),
            "sink": jnp.broadcast_to(attn_sink[l][:, None], (ATTN_HEADS, LANES)),
            "band_bias": band_bias,
        }
        pf, pb = _inproj(xa, ctx_in, modsel[l], norm_w[l], w_in_p, consts, rope_tab, n_ctx_tiles)
        yf = _ssd_forward(pf, pb, consts)
        w_out_b = w_out[l].astype(BF16)
        if last:
            out = _mix_backward(pf, pb, yf, xa, ctx_in, modsel[l], w_out_b, consts, n_ctx_blocks,
                                final_norm_w)
        else:
            xa = _mix_backward(pf, pb, yf, xa, ctx_in, modsel[l], w_out_b, consts, n_ctx_blocks)
            ctx_in = None
    return out
```

```python
import functools

import numpy as np
import jax
import jax.numpy as jnp
from jax import lax
from jax.experimental import pallas as pl
from jax.experimental.pallas import tpu as pltpu

F32 = jnp.float32
BF16 = jnp.bfloat16

EPS = 1e-6
NEG_INF = -1e30
ROPE_BASE = 10000.0
GRID_W = 64

D_MODEL = 1024
CHUNK = 128
HALO = 8
ROW_TILE = 256
SSD_WIDTH = 1024
SSD_HEADS = 16
SSD_GROUPS = 2
SSD_STATE = 128
GROUP_W = SSD_WIDTH // SSD_GROUPS
BC_WIDTH = 2 * SSD_GROUPS * SSD_STATE
SC_WIDTH = 512
ATTN_WIDTH = 512
ATTN_HEADS = 8
ATTN_REP = 4
HEAD_DIM = 64
MIX_WIDTH = 2048
LANES = 128

W_X = 0
W_BC = 1024
W_VC = 1536
W_ZS = 2560
W_BZ = 3584
W_Q = 4608
W_ZA = 5120
W_KV = 5632
W_DT = 5888
PW = 6016

F_XC = 0
F_ZS = 1024
F_GA = 2048
F_DT = 2560
FW = 2688
B_Q = 0
B_BC = 1024
B_YC = 1536
B_KV = 2048
BW = 2304
SCAN_ROWS = 2 * CHUNK

VMEM_LIMIT = 56 * 1024 * 1024


def _sigmoid(v):
    return 1.0 / (1.0 + jnp.exp(-v))


def _silu(v):
    return v * _sigmoid(v)


def _softplus(v):
    return jnp.maximum(v, 0.0) + jnp.log1p(jnp.exp(-jnp.abs(v)))


def _split2(v):
    hi = v.astype(BF16)
    lo = (v - hi.astype(F32)).astype(BF16)
    return hi, lo


def _split3(v):
    hi = v.astype(BF16)
    r = v - hi.astype(F32)
    mid = r.astype(BF16)
    lo = (r - mid.astype(F32)).astype(BF16)
    return hi, mid, lo


def _dot(a, b):
    return jnp.dot(a, b, preferred_element_type=F32)


def _dot_nt(a, b):
    return lax.dot_general(a, b, (((1,), (1,)), ((), ())), preferred_element_type=F32)


def _rmsnorm(x, w):
    ms = jnp.mean(x * x, axis=-1, keepdims=True)
    return x * lax.rsqrt(ms + EPS) * w


def _mod_kernel(c_ref, w_ref, b_ref, o_ref):
    sc = _silu(c_ref[...])
    o_ref[...] = _dot(sc.astype(BF16), w_ref[...].astype(BF16)) + b_ref[...]


def _modulation(cvec, w_mod, b_mod):
    depth = w_mod.shape[0]
    nblk = w_mod.shape[2] // D_MODEL
    return pl.pallas_call(
        _mod_kernel,
        grid=(depth, nblk),
        in_specs=[
            pl.BlockSpec((8, D_MODEL), lambda l, j: (0, 0)),
            pl.BlockSpec((None, D_MODEL, D_MODEL), lambda l, j: (l, 0, j)),
            pl.BlockSpec((None, 1, D_MODEL), lambda l, j: (l, 0, j)),
        ],
        out_specs=pl.BlockSpec((None, 8, D_MODEL), lambda l, j: (l, 0, j)),
        out_shape=jax.ShapeDtypeStruct((depth, 8, w_mod.shape[2]), F32),
        compiler_params=pltpu.CompilerParams(vmem_limit_bytes=VMEM_LIMIT),
        name="modulation",
    )(cvec, w_mod, b_mod.reshape(depth, 1, -1))


def _conv3(t, prev_row, next_row, w_ref):
    n = t.shape[0]
    rows = lax.broadcasted_iota(jnp.int32, t.shape, 0)
    tp = jnp.where(rows == 0, prev_row, pltpu.roll(t, 1, 0))
    tn = jnp.where(rows == n - 1, next_row, pltpu.roll(t, n - 1, 0))
    return w_ref[0:1, :] * tp + w_ref[1:2, :] * t + w_ref[2:3, :] * tn


def _rope(t, cos, sin_signed):
    lane = lax.broadcasted_iota(jnp.int32, t.shape, 1)
    first = (lane % 32) < 16
    swapped = jnp.where(first, pltpu.roll(t, LANES - 16, 1), pltpu.roll(t, 16, 1))
    return t * cos + swapped * sin_signed


def _inproj_kernel(split, n_ctx_tiles, n_tiles, *refs):
    if split:
        ctx_ref, refs = refs[0], refs[1:]
    (x_ref, xp_ref, xn_ref, mod_ref, nw_ref, w_ref, cwx_ref, cbx_ref, cwbc_ref,
     cbbc_ref, cwsc_ref, rope_ref, f_ref, b_ref) = refs
    i = pl.program_id(1)
    has_prev = jnp.logical_and(i != 0, i != n_ctx_tiles).astype(F32)
    has_next = jnp.logical_and(i != n_ctx_tiles - 1, i != n_tiles - 1).astype(F32)

    def norm_mod(x):
        return (_rmsnorm(x, nw_ref[...]) * (1.0 + mod_ref[1:2, :]) + mod_ref[0:1, :]).astype(BF16)

    x_tile = x_ref[...]
    if split:
        x_tile = jnp.where(i < n_ctx_tiles, ctx_ref[...], x_tile)
    h = norm_mod(x_tile)
    hh = jnp.concatenate([norm_mod(xp_ref[...]), norm_mod(xn_ref[...])], axis=0)

    def proj(col, width):
        return _dot(h, w_ref[:, col:col + width])

    def proj_halo(col, width):
        ph = _dot(hh, w_ref[:, col:col + width])
        return ph[HALO - 1:HALO, :] * has_prev, ph[HALO:HALO + 1, :] * has_next

    half = SSD_WIDTH // 2
    for k in range(2):
        pv, nx = proj_halo(W_X + k * half, half)
        y = _conv3(proj(W_X + k * half, half), pv, nx, cwx_ref.at[:, k * half:(k + 1) * half])
        f_ref[:, F_XC + k * half:F_XC + (k + 1) * half] = _silu(y + cbx_ref[:, k * half:(k + 1) * half])
    pv, nx = proj_halo(W_BC, BC_WIDTH)
    y = _conv3(proj(W_BC, BC_WIDTH), pv, nx, cwbc_ref) + cbbc_ref[...]
    b_ref[:, B_BC:B_BC + BC_WIDTH] = _silu(y).astype(BF16)

    w = SC_WIDTH
    vc = proj(W_VC, 2 * w)
    vch = _dot(hh, w_ref[:, W_VC:W_VC + 2 * w])
    cvh = vch[:, 0:w] * vch[:, w:2 * w]
    conv = _conv3(vc[:, 0:w] * vc[:, w:2 * w], cvh[HALO - 1:HALO, :] * has_prev,
                  cvh[HALO:HALO + 1, :] * has_next, cwsc_ref)
    bz = proj(W_BZ, 2 * w)
    b_ref[:, B_YC:B_YC + w] = (bz[:, 0:w] * conv * _silu(bz[:, w:2 * w])).astype(BF16)

    for k in range(2):
        f_ref[:, F_ZS + k * half:F_ZS + (k + 1) * half] = _silu(proj(W_ZS + k * half, half))
    f_ref[:, F_GA:F_GA + ATTN_WIDTH] = _silu(proj(W_ZA, ATTN_WIDTH))
    f_ref[:, F_DT:F_DT + LANES] = proj(W_DT, LANES)

    cos, sin = rope_ref[:, 0:LANES], rope_ref[:, LANES:2 * LANES]
    left = lax.broadcasted_iota(jnp.int32, (ROW_TILE, LANES), 1) < HEAD_DIM
    q = proj(W_Q, ATTN_WIDTH)
    scale = HEAD_DIM ** -0.5
    for j in range(ATTN_WIDTH // LANES):
        qs = _rope(q[:, j * LANES:(j + 1) * LANES], cos, sin) * scale
        qr = pltpu.roll(qs, HEAD_DIM, 1)
        if j // 2 == 0:
            h0, h1 = jnp.where(left, qs, 0.0), jnp.where(left, qr, 0.0)
        else:
            h0, h1 = jnp.where(left, 0.0, qr), jnp.where(left, 0.0, qs)
        b_ref[:, B_Q + 2 * j * LANES:B_Q + (2 * j + 1) * LANES] = h0.astype(BF16)
        b_ref[:, B_Q + (2 * j + 1) * LANES:B_Q + (2 * j + 2) * LANES] = h1.astype(BF16)
    kv = proj(W_KV, 2 * LANES)
    b_ref[:, B_KV:B_KV + LANES] = _rope(kv[:, 0:LANES], cos, sin).astype(BF16)
    b_ref[:, B_KV + LANES:B_KV + 2 * LANES] = kv[:, LANES:2 * LANES].astype(BF16)


def _inproj(xa, ctx, modsel, norm_w, w_in_p, consts, rope_tab, n_ctx_tiles):
    split = ctx is not None
    b, rows, d = xa.shape
    off = n_ctx_tiles if split else 0
    t = rows + off * ROW_TILE
    n_tiles = t // ROW_TILE
    per = ROW_TILE // HALO
    n_halo = rows // HALO

    def const(shape):
        return pl.BlockSpec(shape, lambda bi, i: (0,) * len(shape))

    in_specs = [
            pl.BlockSpec((None, ROW_TILE, d), lambda bi, i: (bi, jnp.maximum(i - off, 0), 0)),
            pl.BlockSpec((None, HALO, d),
                         lambda bi, i: (bi, jnp.maximum((i - off) * per - 1, 0), 0)),
            pl.BlockSpec((None, HALO, d),
                         lambda bi, i: (bi, jnp.clip((i - off + 1) * per, 0, n_halo - 1), 0)),
            pl.BlockSpec((None, None, 3, d),
                         lambda bi, i: (bi, jnp.where(i >= n_ctx_tiles, 1, 0), 0, 0)),
            const((1, d)),
            pl.BlockSpec((d, PW), lambda bi, i: (0, 0), pipeline_mode=pl.Buffered(1)),
            const((3, SSD_WIDTH)), const((1, SSD_WIDTH)), const((3, BC_WIDTH)), const((1, BC_WIDTH)),
            const((3, SC_WIDTH)),
            pl.BlockSpec((ROW_TILE, 2 * LANES), lambda bi, i: (i, 0)),
    ]
    args = [xa, xa, xa, modsel, norm_w.reshape(1, d), w_in_p, consts["cw_x"], consts["cb_x"],
            consts["cw_bc"], consts["cb_bc"], consts["cw_sc"], rope_tab]
    if split:
        assert ctx.shape[1] == ROW_TILE and n_ctx_tiles == 1
        in_specs.insert(0, pl.BlockSpec((None, ROW_TILE, d), lambda bi, i: (bi, 0, 0)))
        args.insert(0, ctx)
    return pl.pallas_call(
        functools.partial(_inproj_kernel, split, n_ctx_tiles, n_tiles),
        grid=(b, n_tiles),
        in_specs=in_specs,
        out_specs=[
            pl.BlockSpec((None, ROW_TILE, FW), lambda bi, i: (bi, i, 0)),
            pl.BlockSpec((None, ROW_TILE, BW), lambda bi, i: (bi, i, 0)),
        ],
        out_shape=[jax.ShapeDtypeStruct((b, t, FW), F32), jax.ShapeDtypeStruct((b, t, BW), BF16)],
        compiler_params=pltpu.CompilerParams(
            dimension_semantics=("parallel", "parallel"), vmem_limit_bytes=VMEM_LIMIT),
        name="in_projection",
    )(*args)


def _scan_chunk(step, direction, n_ctx_chunks, n_chunks):
    if direction == 0:
        return step
    return jnp.where(step < n_ctx_chunks, n_ctx_chunks - 1 - step,
                     n_chunks - 1 - (step - n_ctx_chunks))


def _ssd_prepare(direction, xc, dt_raw, dtb_ref, alog_ref, e_ref):
    lane = lax.broadcasted_iota(jnp.int32, (1, LANES), 1)
    mine = jnp.logical_and(lane >= SSD_HEADS * direction, lane < SSD_HEADS * (direction + 1))
    a_row = jnp.where(mine, -jnp.exp(alog_ref[...]), 0.0)
    dt = _softplus(dt_raw + dtb_ref[...])
    da = dt * a_row

    r_i = lax.broadcasted_iota(jnp.int32, (CHUNK, CHUNK), 0)
    c_i = lax.broadcasted_iota(jnp.int32, (CHUNK, CHUNK), 1)
    tri = (c_i <= r_i) if direction == 0 else (c_i >= r_i)
    tri_b = jnp.where(tri, 1.0, 0.0).astype(BF16)
    u3 = _dot(tri_b, jnp.concatenate(_split3(da), axis=1))
    u = u3[:, 0:LANES] + u3[:, LANES:2 * LANES] + u3[:, 2 * LANES:3 * LANES]
    tot_row = CHUNK - 1 if direction == 0 else 0
    tot = u[tot_row:tot_row + 1, :]
    exp_u = jnp.exp(u)
    exp_d = jnp.exp(tot - u)

    stack = jnp.concatenate((dt.astype(BF16), exp_d.astype(BF16)) + _split2(exp_u), axis=0)
    ex = _dot(stack, e_ref[...])
    xdt = xc * ex[0:CHUNK]
    xdec = xdt * ex[CHUNK:2 * CHUNK]
    eu_e = ex[2 * CHUNK:3 * CHUNK] + ex[3 * CHUNK:4 * CHUNK]
    return dict(direction=direction, tri=tri, u=u, u_t=u.T, xdt=xdt, xdec=xdec, eu_e=eu_e,
                tot_row=tot_row)


def _ssd_output(prep, bc, state_ref):
    direction, tri, u, u_t = prep["direction"], prep["tri"], prep["u"], prep["u_t"]
    xdt, eu_e = prep["xdt"], prep["eu_e"]
    half = lax.broadcasted_iota(jnp.int32, (CHUNK, LANES), 1) < HEAD_DIM
    y_slabs = []
    for g in range(SSD_GROUPS):
        b_g = bc[:, g * SSD_STATE:(g + 1) * SSD_STATE]
        c_g = bc[:, (SSD_GROUPS + g) * SSD_STATE:(SSD_GROUPS + g + 1) * SSD_STATE]
        cb = _dot_nt(c_g, b_g)
        y_off = _dot(c_g, state_ref[g].astype(BF16)) * eu_e[:, g * GROUP_W:(g + 1) * GROUP_W]
        for j in range(GROUP_W // LANES):
            ms = []
            for hh in range(2):
                hl = SSD_HEADS * direction + g * (SSD_HEADS // SSD_GROUPS) + 2 * j + hh
                diff = u[:, hl:hl + 1] - u_t[hl:hl + 1, :]
                ms.append(cb * jnp.exp(jnp.where(tri, diff, -jnp.inf)))
            m2 = jnp.concatenate(ms, axis=1).astype(BF16)
            lo = g * GROUP_W + j * LANES
            slab = xdt[:, lo:lo + LANES]
            xbd = jnp.concatenate([jnp.where(half, slab, 0.0), jnp.where(half, 0.0, slab)],
                                  axis=0).astype(BF16)
            y_slabs.append(_dot(m2, xbd) + y_off[:, j * LANES:(j + 1) * LANES])
    return jnp.concatenate(y_slabs, axis=1)


def _ssd_update_state(prep, bc, state_ref):
    eu_e, xdec, tot_row = prep["eu_e"], prep["xdec"], prep["tot_row"]
    for g in range(SSD_GROUPS):
        b_g = bc[:, g * SSD_STATE:(g + 1) * SSD_STATE]
        decay = eu_e[tot_row:tot_row + 1, g * GROUP_W:(g + 1) * GROUP_W]
        state_ref[g] = state_ref[g] * decay + _dot(
            b_g.astype(F32).T.astype(BF16), xdec[:, g * GROUP_W:(g + 1) * GROUP_W].astype(BF16))


def _ssd_fwd_kernel(xc_ref, bc_ref, dt_ref, dtb_ref, alog_ref, e_ref, o_ref, state_ref):
    @pl.when(pl.program_id(1) == 0)
    def _():
        state_ref[...] = jnp.zeros_like(state_ref)

    for k in range(SCAN_ROWS // CHUNK):
        rows = slice(k * CHUNK, (k + 1) * CHUNK)
        prep = _ssd_prepare(0, xc_ref[rows, :], dt_ref[rows, :], dtb_ref, alog_ref, e_ref)
        bc = bc_ref[rows, :]
        o_ref[rows, :] = _ssd_output(prep, bc, state_ref)
        _ssd_update_state(prep, bc, state_ref)


def _ssd_forward(pf, pb, consts):
    b, t, _ = pf.shape
    return pl.pallas_call(
        _ssd_fwd_kernel,
        grid=(b, t // SCAN_ROWS),
        in_specs=[
            pl.BlockSpec((None, SCAN_ROWS, SSD_WIDTH), lambda bi, s: (bi, s, F_XC // SSD_WIDTH)),
            pl.BlockSpec((None, SCAN_ROWS, BC_WIDTH), lambda bi, s: (bi, s, B_BC // BC_WIDTH)),
            pl.BlockSpec((None, SCAN_ROWS, LANES), lambda bi, s: (bi, s, F_DT // LANES)),
            pl.BlockSpec((None, 1, LANES), lambda bi, s: (0, 0, 0)),
            pl.BlockSpec((None, 1, LANES), lambda bi, s: (0, 0, 0)),
            pl.BlockSpec((None, LANES, SSD_WIDTH), lambda bi, s: (0, 0, 0)),
        ],
        out_specs=pl.BlockSpec((None, SCAN_ROWS, SSD_WIDTH), lambda bi, s: (bi, s, 0)),
        out_shape=jax.ShapeDtypeStruct((b, t, SSD_WIDTH), F32),
        scratch_shapes=[pltpu.VMEM((SSD_GROUPS, SSD_STATE, GROUP_W), F32)],
        compiler_params=pltpu.CompilerParams(
            dimension_semantics=("parallel", "arbitrary"), vmem_limit_bytes=VMEM_LIMIT),
        name="ssd_forward",
    )(pf, pb, pf, consts["dt_bias"], consts["a_log"], consts["expand"])


def _attend(q_ref, rows, k_all, v_all, bias, sink_ref):
    left = lax.broadcasted_iota(jnp.int32, (CHUNK, LANES), 1) < HEAD_DIM
    n_pairs = ATTN_HEADS // 2
    scores = []
    for j in range(n_pairs):
        q2 = jnp.concatenate([q_ref[rows, 2 * j * LANES:(2 * j + 1) * LANES],
                              q_ref[rows, (2 * j + 1) * LANES:(2 * j + 2) * LANES]], axis=0)
        s = _dot_nt(q2, k_all)
        scores.append(s if bias is None else s + bias)
    n_keys = k_all.shape[0]
    probs, sink_terms = [], []
    for j, s in enumerate(scores):
        sk = jnp.concatenate([
            jnp.broadcast_to(sink_ref[2 * j:2 * j + 1, :], (CHUNK, LANES)),
            jnp.broadcast_to(sink_ref[2 * j + 1:2 * j + 2, :], (CHUNK, LANES))], axis=0)
        m = jnp.maximum(jnp.broadcast_to(jnp.max(s, axis=-1, keepdims=True), (2 * CHUNK, LANES)), sk)
        probs.append(jnp.concatenate(
            [jnp.exp(s[:, i * LANES:(i + 1) * LANES] - m).astype(BF16) for i in range(n_keys // LANES)],
            axis=1))
        sink_terms.append(jnp.exp(sk - m))
    v_ones = jnp.concatenate([v_all, jnp.ones((n_keys, LANES), BF16)], axis=1)
    outs = []
    for j in range(n_pairs):
        pv = _dot(probs[j], v_ones)
        o = pv[:, 0:LANES] / (pv[:, LANES:2 * LANES] + sink_terms[j])
        o0, o1 = o[0:CHUNK], o[CHUNK:2 * CHUNK]
        if j // 2 == 0:
            outs.append(jnp.where(left, o0, pltpu.roll(o1, HEAD_DIM, 1)))
        else:
            outs.append(jnp.where(left, pltpu.roll(o0, HEAD_DIM, 1), o1))
    return outs


def _mix_bwd_kernel(final, split, n_ctx_blocks, n_blocks, *refs):
    if split:
        ctx_ref, refs = refs[0], refs[1:]
    (xc_ref, bc_ref, dt_ref, zs_ref, yf_ref, yc_ref, ga_ref, q_ref, kv_ref, kvp_ref, kvn_ref,
     kvc_ref, x_ref, mod_ref, w_ref, dtb_ref, alog_ref, e_ref, dsk_ref, nw_ref, sink_ref,
     bias_ref) = refs[:22]
    if final:
        fw_ref, o_ref, state_ref, ycat_ref = refs[22:]
    else:
        o_ref, state_ref, ycat_ref = refs[22:]

    step = pl.program_id(1)
    blk = _scan_chunk(step, 1, n_ctx_blocks, n_blocks)
    per = SCAN_ROWS // CHUNK
    n_ctx_chunks, n_chunks = per * n_ctx_blocks, per * n_blocks

    @pl.when(step == 0)
    def _():
        state_ref[...] = jnp.zeros_like(state_ref)

    col = lax.broadcasted_iota(jnp.int32, (1, bias_ref.shape[1]), 1)
    for k in reversed(range(per)):
        rows = slice(k * CHUNK, (k + 1) * CHUNK)
        c = blk * per + k
        xc = xc_ref[rows, :]
        bc = bc_ref[rows, :]
        prep = _ssd_prepare(1, xc, dt_ref[rows, :], dtb_ref, alog_ref, e_ref)

        y = _ssd_output(prep, bc, state_ref) + yf_ref[rows, :] + xc * dsk_ref[...]
        gated = y * zs_ref[rows, :]
        for g in range(SSD_GROUPS):
            gg = gated[:, g * GROUP_W:(g + 1) * GROUP_W]
            ms = jnp.mean(gg * gg, axis=-1, keepdims=True)
            ycat_ref[rows, g * GROUP_W:(g + 1) * GROUP_W] = (
                gg * lax.rsqrt(ms + EPS) * nw_ref[:, g * GROUP_W:(g + 1) * GROUP_W]).astype(BF16)
        _ssd_update_state(prep, bc, state_ref)

        halves = [kvp_ref] + [kv_ref.at[i * CHUNK:(i + 1) * CHUNK, :] for i in range(per)] + [kvn_ref]
        win = halves[k:k + 3] + [kvc_ref]
        k_all = jnp.concatenate([r[:, 0:LANES] for r in win], axis=0)
        v_all = jnp.concatenate([r[:, LANES:2 * LANES] for r in win], axis=0)
        latent = c >= n_ctx_chunks
        no_prev = jnp.where(c > n_ctx_chunks, 0.0, NEG_INF)
        no_cur = jnp.where(latent, 0.0, NEG_INF)
        no_next = jnp.where(jnp.logical_and(latent, c < n_chunks - 1), 0.0, NEG_INF)
        edge = jnp.where(col < CHUNK, no_prev,
                         jnp.where(col < 2 * CHUNK, no_cur, jnp.where(col < 3 * CHUNK, no_next, 0.0)))
        slabs = _attend(q_ref, rows, k_all, v_all, bias_ref[...] + edge, sink_ref)
        ycat_ref[rows, SSD_WIDTH + SC_WIDTH:MIX_WIDTH] = (
            jnp.concatenate(slabs, axis=1) * ga_ref[rows, :]).astype(BF16)

    acc = _dot(ycat_ref[:, 0:SSD_WIDTH], w_ref[0:SSD_WIDTH, :])
    acc = acc + _dot(yc_ref[...], w_ref[SSD_WIDTH:SSD_WIDTH + SC_WIDTH, :])
    acc = acc + _dot(ycat_ref[:, SSD_WIDTH + SC_WIDTH:MIX_WIDTH], w_ref[SSD_WIDTH + SC_WIDTH:MIX_WIDTH, :])
    x_in = x_ref[...]
    if split:
        x_in = jnp.where(blk < n_ctx_blocks, ctx_ref[...], x_in)
    xn = x_in + mod_ref[2:3, :] * acc
    if final:
        xn = _rmsnorm(xn, fw_ref[...])
    o_ref[...] = xn


def _mix_backward(pf, pb, yf, xa, ctx, modsel, w_out_b, consts, n_ctx_blocks, final_norm_w=None):
    split = ctx is not None
    b, t, _ = pf.shape
    d = xa.shape[-1]
    n_blocks = t // SCAN_ROWS
    per = SCAN_ROWS // CHUNK
    n_chunks = per * n_blocks
    final = final_norm_w is not None
    n_ctx = n_ctx_blocks * SCAN_ROWS
    x_off = n_ctx_blocks if split else 0

    def bidx(s):
        return _scan_chunk(s, 1, n_ctx_blocks, n_blocks)

    def blk(width, offset):
        return pl.BlockSpec((None, SCAN_ROWS, width), lambda bi, s: (bi, bidx(s), offset // width))

    def const(shape):
        return pl.BlockSpec(shape, lambda bi, s: (0,) * len(shape))

    kvw = 2 * LANES
    in_specs = [
        blk(SSD_WIDTH, F_XC), blk(BC_WIDTH, B_BC), blk(LANES, F_DT), blk(SSD_WIDTH, F_ZS),
        blk(SSD_WIDTH, 0),
        blk(SC_WIDTH, B_YC), blk(ATTN_WIDTH, F_GA), blk(2 * ATTN_WIDTH, B_Q),
        blk(kvw, B_KV),
        pl.BlockSpec((None, CHUNK, kvw),
                     lambda bi, s: (bi, jnp.maximum(bidx(s) * per - 1, 0), B_KV // kvw)),
        pl.BlockSpec((None, CHUNK, kvw),
                     lambda bi, s: (bi, jnp.minimum((bidx(s) + 1) * per, n_chunks - 1), B_KV // kvw)),
        pl.BlockSpec((None, n_ctx, kvw), lambda bi, s: (bi, 0, B_KV // kvw)),
        pl.BlockSpec((None, SCAN_ROWS, d),
                     lambda bi, s: (bi, jnp.maximum(bidx(s) - x_off, 0), 0)),
        pl.BlockSpec((None, None, 3, d),
                     lambda bi, s: (bi, jnp.where(bidx(s) >= n_ctx_blocks, 1, 0), 0, 0)),
        pl.BlockSpec((MIX_WIDTH, d), lambda bi, s: (0, 0), pipeline_mode=pl.Buffered(1)),
        pl.BlockSpec((None, 1, LANES), lambda bi, s: (1, 0, 0)),
        pl.BlockSpec((None, 1, LANES), lambda bi, s: (1, 0, 0)),
        pl.BlockSpec((None, LANES, SSD_WIDTH), lambda bi, s: (1, 0, 0)),
        const((1, SSD_WIDTH)), const((1, SSD_WIDTH)), const((ATTN_HEADS, LANES)),
        const((2 * CHUNK, 3 * CHUNK + n_ctx)),
    ]
    args = [pf, pb, pf, pf, yf, pb, pf, pb, pb, pb, pb, pb, xa, modsel, w_out_b,
            consts["dt_bias"], consts["a_log"], consts["expand"], consts["d_skip"],
            consts["ssd_norm_w"], consts["sink"], consts["band_bias"]]
    if split:
        assert ctx.shape[1] == SCAN_ROWS and n_ctx_blocks == 1
        in_specs.insert(0, pl.BlockSpec((None, SCAN_ROWS, d), lambda bi, s: (bi, 0, 0)))
        args.insert(0, ctx)
    if final:
        in_specs.append(const((1, d)))
        args.append(final_norm_w.reshape(1, d))
        out_spec = pl.BlockSpec(
            (None, SCAN_ROWS, d),
            lambda bi, s: (bi, jnp.where(bidx(s) >= n_ctx_blocks, bidx(s), n_blocks - 1) - n_ctx_blocks, 0))
        out_shape = jax.ShapeDtypeStruct((b, t - n_ctx, d), F32)
    else:
        out_spec = blk(d, 0)
        out_shape = jax.ShapeDtypeStruct((b, t, d), F32)
    return pl.pallas_call(
        functools.partial(_mix_bwd_kernel, final, split, n_ctx_blocks, n_blocks),
        grid=(b, n_blocks),
        in_specs=in_specs,
        out_specs=out_spec,
        out_shape=out_shape,
        scratch_shapes=[pltpu.VMEM((SSD_GROUPS, SSD_STATE, GROUP_W), F32),
                        pltpu.VMEM((SCAN_ROWS, MIX_WIDTH), BF16)],
        compiler_params=pltpu.CompilerParams(
            dimension_semantics=("parallel", "arbitrary"), vmem_limit_bytes=VMEM_LIMIT),
        name="mix_backward_final" if final else "mix_backward",
    )(*args)


def _permute_in_columns(w):
    d = w.shape[0]
    sc0 = 2592
    at0 = sc0 + 4 * SC_WIDTH
    parts = [
        w[:, 1024:2048],
        w[:, 2048:2560],
        w[:, sc0:sc0 + 1024],
        w[:, 0:1024],
        w[:, sc0 + 1024:at0],
        w[:, at0:at0 + 512],
        w[:, at0 + 768:at0 + 1280],
        w[:, at0 + 512:at0 + 768],
        w[:, 2560:2592],
        jnp.zeros((d, PW - W_DT - 2 * SSD_HEADS), w.dtype),
    ]
    return jnp.concatenate(parts, axis=1)


def _rope_table(n_ctx, n_lat):
    n_rows = n_lat // GRID_W
    rows = np.repeat(np.arange(n_rows), GRID_W).astype(np.float32)
    cols = np.tile(np.arange(GRID_W), n_rows).astype(np.float32)
    axis_dim = HEAD_DIM // 2
    inv_freq = np.float32(ROPE_BASE) ** (-np.arange(0, axis_dim, 2, dtype=np.float32) / np.float32(axis_dim))
    ang = np.concatenate([rows[:, None] * inv_freq, cols[:, None] * inv_freq], axis=-1)
    cos, sin = np.cos(ang).astype(np.float32), np.sin(ang).astype(np.float32)
    q = HEAD_DIM // 4
    cos_h = np.concatenate([cos[:, 0:q], cos[:, 0:q], cos[:, q:2 * q], cos[:, q:2 * q]], axis=1)
    sin_h = np.concatenate([-sin[:, 0:q], sin[:, 0:q], -sin[:, q:2 * q], sin[:, q:2 * q]], axis=1)
    tab = np.concatenate([cos_h, cos_h, sin_h, sin_h], axis=1)
    ident = np.concatenate([np.ones((n_ctx, LANES), np.float32), np.zeros((n_ctx, LANES), np.float32)], axis=1)
    return jnp.asarray(np.concatenate([ident, tab], axis=0))


def _expand_matrix():
    e = np.zeros((2, LANES, SSD_WIDTH), np.float32)
    for d in range(2):
        for h in range(SSD_HEADS):
            e[d, d * SSD_HEADS + h, h * HEAD_DIM:(h + 1) * HEAD_DIM] = 1.0
    return jnp.asarray(e, BF16)


def _band_bias(n_ctx):
    row = np.arange(2 * CHUNK)[:, None] % CHUNK
    col = np.arange(3 * CHUNK + n_ctx)[None, :]
    ok = (col >= 3 * CHUNK) | ((col >= row) & (col <= row + 2 * CHUNK))
    return jnp.asarray(np.where(ok, 0.0, NEG_INF), F32)


def _dir_rows(v):
    out = jnp.zeros((2, 1, LANES), F32)
    for d in range(2):
        out = out.at[d, 0, d * SSD_HEADS:(d + 1) * SSD_HEADS].set(v[d])
    return out


def kernel(x, c, ctx, c_ctx, norm_w, w_mod, b_mod, w_in, ssd_conv_w, ssd_conv_b, ssd_dt_bias,
           ssd_a_log, ssd_d, ssd_norm_w, sc_conv_w, attn_sink, w_out, final_norm_w):
    b, n_lat, d = x.shape
    n_ctx = ctx.shape[1]
    depth = w_in.shape[0]
    assert d == D_MODEL and n_ctx % ROW_TILE == 0 and n_lat % ROW_TILE == 0 and b <= 7
    n_ctx_chunks = n_ctx // CHUNK
    n_ctx_tiles = n_ctx // ROW_TILE

    cvec = jnp.zeros((8, d), F32).at[0:b].set(c).at[b].set(c_ctx)
    mod = _modulation(cvec, w_mod, b_mod)
    mod = mod.reshape(depth, 8, 3, d)
    modsel = jnp.stack([jnp.broadcast_to(mod[:, b][:, None], (depth, b, 3, d)), mod[:, 0:b]], axis=2)

    xa, ctx_in = x, ctx
    assert n_ctx == ROW_TILE == SCAN_ROWS
    n_ctx_blocks = n_ctx // SCAN_ROWS
    rope_tab = _rope_table(n_ctx, n_lat)
    expand = _expand_matrix()
    band_bias = _band_bias(n_ctx)

    out = None
    for l in range(depth):
        last = l == depth - 1
        w_in_p = _permute_in_columns(w_in[l].astype(BF16))
        cw = ssd_conv_w[l]
        cbias = ssd_conv_b[l]
        consts = {
            "cw_x": cw[:, 0:SSD_WIDTH], "cb_x": cbias[0:SSD_WIDTH].reshape(1, -1),
            "cw_bc": cw[:, SSD_WIDTH:], "cb_bc": cbias[SSD_WIDTH:].reshape(1, -1),
            "cw_sc": sc_conv_w[l],
            "dt_bias": _dir_rows(ssd_dt_bias[l]), "a_log": _dir_rows(ssd_a_log[l]),
            "expand": expand,
            "d_skip": jnp.repeat(ssd_d[l], HEAD_DIM).reshape(1, -1),
            "ssd_norm_w": ssd_norm_w[l].reshape(1, -1),
            "sink": jnp.broadcast_to(attn_sink[l][:, None], (ATTN_HEADS, LANES)),
            "band_bias": band_bias,
        }
        pf, pb = _inproj(xa, ctx_in, modsel[l], norm_w[l], w_in_p, consts, rope_tab, n_ctx_tiles)
        yf = _ssd_forward(pf, pb, consts)
        w_out_b = w_out[l].astype(BF16)
        if last:
            out = _mix_backward(pf, pb, yf, xa, ctx_in, modsel[l], w_out_b, consts, n_ctx_blocks,
                                final_norm_w)
        else:
            xa = _mix_backward(pf, pb, yf, xa, ctx_in, modsel[l], w_out_b, consts, n_ctx_blocks)
            ctx_in = None
    return out
```

```python
import functools

import numpy as np
import jax
import jax.numpy as jnp
from jax import lax
from jax.experimental import pallas as pl
from jax.experimental.pallas import tpu as pltpu

F32 = jnp.float32
BF16 = jnp.bfloat16

EPS = 1e-6
NEG_INF = -1e30
ROPE_BASE = 10000.0
GRID_W = 64

D_MODEL = 1024
CHUNK = 128
HALO = 8
ROW_TILE = 256
SSD_WIDTH = 1024
SSD_HEADS = 16
SSD_GROUPS = 2
SSD_STATE = 128
GROUP_W = SSD_WIDTH // SSD_GROUPS
BC_WIDTH = 2 * SSD_GROUPS * SSD_STATE
SC_WIDTH = 512
ATTN_WIDTH = 512
ATTN_HEADS = 8
ATTN_REP = 4
HEAD_DIM = 64
MIX_WIDTH = 2048
LANES = 128

W_X = 0
W_BC = 1024
W_VC = 1536
W_ZS = 2560
W_BZ = 3584
W_Q = 4608
W_ZA = 5120
W_KV = 5632
W_DT = 5888
PW = 6016
_W_GROUPS = ((1024, W_X, 1024), (2048, W_BC, 512), (2592, W_VC, 1024), (0, W_ZS, 1024),
             (3616, W_BZ, 1024), (4640, W_Q, 512), (5408, W_ZA, 512), (5152, W_KV, 256),
             (2560, W_DT, 32))
IN_COLS = 5920

F_XC = 0
F_ZS = 1024
F_GA = 2048
F_DT = 2560
FW = 2688
B_Q = 0
B_BC = 1024
B_YC = 1536
B_KV = 2048
BW = 2304
SCAN_ROWS = 2 * CHUNK

VMEM_LIMIT = 56 * 1024 * 1024


def _sigmoid(v):
    return 1.0 / (1.0 + jnp.exp(-v))


def _silu(v):
    return v * _sigmoid(v)


def _softplus(v):
    return jnp.maximum(v, 0.0) + jnp.log1p(jnp.exp(-jnp.abs(v)))


def _split2(v):
    hi = v.astype(BF16)
    lo = (v - hi.astype(F32)).astype(BF16)
    return hi, lo


def _split3(v):
    hi = v.astype(BF16)
    r = v - hi.astype(F32)
    mid = r.astype(BF16)
    lo = (r - mid.astype(F32)).astype(BF16)
    return hi, mid, lo


def _dot(a, b):
    return jnp.dot(a, b, preferred_element_type=F32)


def _dot_nt(a, b):
    return lax.dot_general(a, b, (((1,), (1,)), ((), ())), preferred_element_type=F32)


def _rmsnorm(x, w):
    ms = jnp.mean(x * x, axis=-1, keepdims=True)
    return x * lax.rsqrt(ms + EPS) * w


def _mod_kernel(c_ref, w_ref, b_ref, o_ref):
    sc = _silu(c_ref[...])
    o_ref[...] = _dot(sc.astype(BF16), w_ref[...].astype(BF16)) + b_ref[...]


def _modulation(cvec, w_mod, b_mod):
    depth = w_mod.shape[0]
    nblk = w_mod.shape[2] // D_MODEL
    return pl.pallas_call(
        _mod_kernel,
        grid=(depth, nblk),
        in_specs=[
            pl.BlockSpec((8, D_MODEL), lambda l, j: (0, 0)),
            pl.BlockSpec((None, D_MODEL, D_MODEL), lambda l, j: (l, 0, j)),
            pl.BlockSpec((None, 1, D_MODEL), lambda l, j: (l, 0, j)),
        ],
        out_specs=pl.BlockSpec((None, 8, D_MODEL), lambda l, j: (l, 0, j)),
        out_shape=jax.ShapeDtypeStruct((depth, 8, w_mod.shape[2]), F32),
        compiler_params=pltpu.CompilerParams(vmem_limit_bytes=VMEM_LIMIT),
        name="modulation",
    )(cvec, w_mod, b_mod.reshape(depth, 1, -1))


def _conv3(t, prev_row, next_row, w_ref):
    n = t.shape[0]
    rows = lax.broadcasted_iota(jnp.int32, t.shape, 0)
    tp = jnp.where(rows == 0, prev_row, pltpu.roll(t, 1, 0))
    tn = jnp.where(rows == n - 1, next_row, pltpu.roll(t, n - 1, 0))
    return w_ref[0:1, :] * tp + w_ref[1:2, :] * t + w_ref[2:3, :] * tn


def _rope(t, cos, sin_signed):
    lane = lax.broadcasted_iota(jnp.int32, t.shape, 1)
    first = (lane % 32) < 16
    swapped = jnp.where(first, pltpu.roll(t, LANES - 16, 1), pltpu.roll(t, 16, 1))
    return t * cos + swapped * sin_signed


def _inproj_kernel(split, n_ctx_tiles, n_tiles, *refs):
    if split:
        ctx_ref, refs = refs[0], refs[1:]
    (x_ref, xp_ref, xn_ref, mod_ref, nw_ref, wraw_ref, cwx_ref, cbx_ref, cwbc_ref,
     cbbc_ref, cwsc_ref, rope_ref, f_ref, b_ref, w_ref) = refs
    i = pl.program_id(1)

    @pl.when(jnp.logical_and(pl.program_id(0) == 0, i == 0))
    def _():
        for src, dst, width in _W_GROUPS:
            for c0 in range(0, width, 512):
                cw = min(512, width - c0)
                w_ref[:, dst + c0:dst + c0 + cw] = wraw_ref[:, src + c0:src + c0 + cw].astype(BF16)
        pad0 = W_DT + 2 * SSD_HEADS
        w_ref[:, pad0:PW] = jnp.zeros((D_MODEL, PW - pad0), BF16)

    has_prev = jnp.logical_and(i != 0, i != n_ctx_tiles).astype(F32)
    has_next = jnp.logical_and(i != n_ctx_tiles - 1, i != n_tiles - 1).astype(F32)

    def norm_mod(x):
        return (_rmsnorm(x, nw_ref[...]) * (1.0 + mod_ref[1:2, :]) + mod_ref[0:1, :]).astype(BF16)

    x_tile = x_ref[...]
    if split:
        x_tile = jnp.where(i < n_ctx_tiles, ctx_ref[...], x_tile)
    h = norm_mod(x_tile)
    hh = jnp.concatenate([norm_mod(xp_ref[...]), norm_mod(xn_ref[...])], axis=0)

    def proj(col, width):
        return _dot(h, w_ref[:, col:col + width])

    def proj_halo(col, width):
        ph = _dot(hh, w_ref[:, col:col + width])
        return ph[HALO - 1:HALO, :] * has_prev, ph[HALO:HALO + 1, :] * has_next

    half = SSD_WIDTH // 2
    for k in range(2):
        pv, nx = proj_halo(W_X + k * half, half)
        y = _conv3(proj(W_X + k * half, half), pv, nx, cwx_ref.at[:, k * half:(k + 1) * half])
        f_ref[:, F_XC + k * half:F_XC + (k + 1) * half] = _silu(y + cbx_ref[:, k * half:(k + 1) * half])
    pv, nx = proj_halo(W_BC, BC_WIDTH)
    y = _conv3(proj(W_BC, BC_WIDTH), pv, nx, cwbc_ref) + cbbc_ref[...]
    b_ref[:, B_BC:B_BC + BC_WIDTH] = _silu(y).astype(BF16)

    w = SC_WIDTH
    vc = proj(W_VC, 2 * w)
    vch = _dot(hh, w_ref[:, W_VC:W_VC + 2 * w])
    cvh = vch[:, 0:w] * vch[:, w:2 * w]
    conv = _conv3(vc[:, 0:w] * vc[:, w:2 * w], cvh[HALO - 1:HALO, :] * has_prev,
                  cvh[HALO:HALO + 1, :] * has_next, cwsc_ref)
    bz = proj(W_BZ, 2 * w)
    b_ref[:, B_YC:B_YC + w] = (bz[:, 0:w] * conv * _silu(bz[:, w:2 * w])).astype(BF16)

    for k in range(2):
        f_ref[:, F_ZS + k * half:F_ZS + (k + 1) * half] = _silu(proj(W_ZS + k * half, half))
    f_ref[:, F_GA:F_GA + ATTN_WIDTH] = _silu(proj(W_ZA, ATTN_WIDTH))
    f_ref[:, F_DT:F_DT + LANES] = proj(W_DT, LANES)

    cos, sin = rope_ref[:, 0:LANES], rope_ref[:, LANES:2 * LANES]
    left = lax.broadcasted_iota(jnp.int32, (ROW_TILE, LANES), 1) < HEAD_DIM
    q = proj(W_Q, ATTN_WIDTH)
    scale = HEAD_DIM ** -0.5
    for j in range(ATTN_WIDTH // LANES):
        qs = _rope(q[:, j * LANES:(j + 1) * LANES], cos, sin) * scale
        qr = pltpu.roll(qs, HEAD_DIM, 1)
        if j // 2 == 0:
            h0, h1 = jnp.where(left, qs, 0.0), jnp.where(left, qr, 0.0)
        else:
            h0, h1 = jnp.where(left, 0.0, qr), jnp.where(left, 0.0, qs)
        b_ref[:, B_Q + 2 * j * LANES:B_Q + (2 * j + 1) * LANES] = h0.astype(BF16)
        b_ref[:, B_Q + (2 * j + 1) * LANES:B_Q + (2 * j + 2) * LANES] = h1.astype(BF16)
    kv = proj(W_KV, 2 * LANES)
    b_ref[:, B_KV:B_KV + LANES] = _rope(kv[:, 0:LANES], cos, sin).astype(BF16)
    b_ref[:, B_KV + LANES:B_KV + 2 * LANES] = kv[:, LANES:2 * LANES].astype(BF16)


def _inproj(xa, ctx, modsel, norm_w, w_in_p, consts, rope_tab, n_ctx_tiles):
    split = ctx is not None
    b, rows, d = xa.shape
    off = n_ctx_tiles if split else 0
    t = rows + off * ROW_TILE
    n_tiles = t // ROW_TILE
    per = ROW_TILE // HALO
    n_halo = rows // HALO

    def const(shape):
        return pl.BlockSpec(shape, lambda bi, i: (0,) * len(shape))

    in_specs = [
            pl.BlockSpec((None, ROW_TILE, d), lambda bi, i: (bi, jnp.maximum(i - off, 0), 0)),
            pl.BlockSpec((None, HALO, d),
                         lambda bi, i: (bi, jnp.maximum((i - off) * per - 1, 0), 0)),
            pl.BlockSpec((None, HALO, d),
                         lambda bi, i: (bi, jnp.clip((i - off + 1) * per, 0, n_halo - 1), 0)),
            pl.BlockSpec((None, None, 3, d),
                         lambda bi, i: (bi, jnp.where(i >= n_ctx_tiles, 1, 0), 0, 0)),
            const((1, d)),
            pl.BlockSpec((d, IN_COLS), lambda bi, i: (0, 0), pipeline_mode=pl.Buffered(1)),
            const((3, SSD_WIDTH)), const((1, SSD_WIDTH)), const((3, BC_WIDTH)), const((1, BC_WIDTH)),
            const((3, SC_WIDTH)),
            pl.BlockSpec((ROW_TILE, 2 * LANES), lambda bi, i: (i, 0)),
    ]
    args = [xa, xa, xa, modsel, norm_w.reshape(1, d), w_in_p, consts["cw_x"], consts["cb_x"],
            consts["cw_bc"], consts["cb_bc"], consts["cw_sc"], rope_tab]
    if split:
        assert ctx.shape[1] == ROW_TILE and n_ctx_tiles == 1
        in_specs.insert(0, pl.BlockSpec((None, ROW_TILE, d), lambda bi, i: (bi, 0, 0)))
        args.insert(0, ctx)
    return pl.pallas_call(
        functools.partial(_inproj_kernel, split, n_ctx_tiles, n_tiles),
        grid=(b, n_tiles),
        in_specs=in_specs,
        out_specs=[
            pl.BlockSpec((None, ROW_TILE, FW), lambda bi, i: (bi, i, 0)),
            pl.BlockSpec((None, ROW_TILE, BW), lambda bi, i: (bi, i, 0)),
        ],
        out_shape=[jax.ShapeDtypeStruct((b, t, FW), F32), jax.ShapeDtypeStruct((b, t, BW), BF16)],
        scratch_shapes=[pltpu.VMEM((d, PW), BF16)],
        compiler_params=pltpu.CompilerParams(
            dimension_semantics=("arbitrary", "arbitrary"), vmem_limit_bytes=VMEM_LIMIT),
        name="in_projection",
    )(*args)


def _scan_chunk(step, direction, n_ctx_chunks, n_chunks):
    if direction == 0:
        return step
    return jnp.where(step < n_ctx_chunks, n_ctx_chunks - 1 - step,
                     n_chunks - 1 - (step - n_ctx_chunks))


def _ssd_prepare(direction, xc, dt_raw, dtb_ref, alog_ref, e_ref):
    lane = lax.broadcasted_iota(jnp.int32, (1, LANES), 1)
    mine = jnp.logical_and(lane >= SSD_HEADS * direction, lane < SSD_HEADS * (direction + 1))
    a_row = jnp.where(mine, -jnp.exp(alog_ref[...]), 0.0)
    dt = _softplus(dt_raw + dtb_ref[...])
    da = dt * a_row

    r_i = lax.broadcasted_iota(jnp.int32, (CHUNK, CHUNK), 0)
    c_i = lax.broadcasted_iota(jnp.int32, (CHUNK, CHUNK), 1)
    tri = (c_i <= r_i) if direction == 0 else (c_i >= r_i)
    tri_b = jnp.where(tri, 1.0, 0.0).astype(BF16)
    u3 = _dot(tri_b, jnp.concatenate(_split3(da), axis=1))
    u = u3[:, 0:LANES] + u3[:, LANES:2 * LANES] + u3[:, 2 * LANES:3 * LANES]
    tot_row = CHUNK - 1 if direction == 0 else 0
    tot = u[tot_row:tot_row + 1, :]
    exp_u = jnp.exp(u)
    exp_d = jnp.exp(tot - u)

    stack = jnp.concatenate((dt.astype(BF16), exp_d.astype(BF16)) + _split2(exp_u), axis=0)
    ex = _dot(stack, e_ref[...])
    xdt = xc * ex[0:CHUNK]
    xdec = xdt * ex[CHUNK:2 * CHUNK]
    eu_e = ex[2 * CHUNK:3 * CHUNK] + ex[3 * CHUNK:4 * CHUNK]
    return dict(direction=direction, tri=tri, u=u, u_t=u.T, xdt=xdt, xdec=xdec, eu_e=eu_e,
                tot_row=tot_row)


def _ssd_output(prep, bc, state_ref):
    direction, tri, u, u_t = prep["direction"], prep["tri"], prep["u"], prep["u_t"]
    xdt, eu_e = prep["xdt"], prep["eu_e"]
    half = lax.broadcasted_iota(jnp.int32, (CHUNK, LANES), 1) < HEAD_DIM
    y_slabs = []
    for g in range(SSD_GROUPS):
        b_g = bc[:, g * SSD_STATE:(g + 1) * SSD_STATE]
        c_g = bc[:, (SSD_GROUPS + g) * SSD_STATE:(SSD_GROUPS + g + 1) * SSD_STATE]
        cb = _dot_nt(c_g, b_g)
        y_off = _dot(c_g, state_ref[g].astype(BF16)) * eu_e[:, g * GROUP_W:(g + 1) * GROUP_W]
        for j in range(GROUP_W // LANES):
            ms = []
            for hh in range(2):
                hl = SSD_HEADS * direction + g * (SSD_HEADS // SSD_GROUPS) + 2 * j + hh
                diff = u[:, hl:hl + 1] - u_t[hl:hl + 1, :]
                ms.append(cb * jnp.exp(jnp.where(tri, diff, -jnp.inf)))
            m2 = jnp.concatenate(ms, axis=1).astype(BF16)
            lo = g * GROUP_W + j * LANES
            slab = xdt[:, lo:lo + LANES]
            xbd = jnp.concatenate([jnp.where(half, slab, 0.0), jnp.where(half, 0.0, slab)],
                                  axis=0).astype(BF16)
            y_slabs.append(_dot(m2, xbd) + y_off[:, j * LANES:(j + 1) * LANES])
    return jnp.concatenate(y_slabs, axis=1)


def _ssd_update_state(prep, bc, state_ref):
    eu_e, xdec, tot_row = prep["eu_e"], prep["xdec"], prep["tot_row"]
    for g in range(SSD_GROUPS):
        b_g = bc[:, g * SSD_STATE:(g + 1) * SSD_STATE]
        decay = eu_e[tot_row:tot_row + 1, g * GROUP_W:(g + 1) * GROUP_W]
        state_ref[g] = state_ref[g] * decay + _dot(
            b_g.astype(F32).T.astype(BF16), xdec[:, g * GROUP_W:(g + 1) * GROUP_W].astype(BF16))


def _ssd_fwd_kernel(xc_ref, bc_ref, dt_ref, dtb_ref, alog_ref, e_ref, o_ref, state_ref):
    @pl.when(pl.program_id(1) == 0)
    def _():
        state_ref[...] = jnp.zeros_like(state_ref)

    for k in range(SCAN_ROWS // CHUNK):
        rows = slice(k * CHUNK, (k + 1) * CHUNK)
        prep = _ssd_prepare(0, xc_ref[rows, :], dt_ref[rows, :], dtb_ref, alog_ref, e_ref)
        bc = bc_ref[rows, :]
        o_ref[rows, :] = _ssd_output(prep, bc, state_ref)
        _ssd_update_state(prep, bc, state_ref)


def _ssd_forward(pf, pb, consts):
    b, t, _ = pf.shape
    return pl.pallas_call(
        _ssd_fwd_kernel,
        grid=(b, t // SCAN_ROWS),
        in_specs=[
            pl.BlockSpec((None, SCAN_ROWS, SSD_WIDTH), lambda bi, s: (bi, s, F_XC // SSD_WIDTH)),
            pl.BlockSpec((None, SCAN_ROWS, BC_WIDTH), lambda bi, s: (bi, s, B_BC // BC_WIDTH)),
            pl.BlockSpec((None, SCAN_ROWS, LANES), lambda bi, s: (bi, s, F_DT // LANES)),
            pl.BlockSpec((None, 1, LANES), lambda bi, s: (0, 0, 0)),
            pl.BlockSpec((None, 1, LANES), lambda bi, s: (0, 0, 0)),
            pl.BlockSpec((None, LANES, SSD_WIDTH), lambda bi, s: (0, 0, 0)),
        ],
        out_specs=pl.BlockSpec((None, SCAN_ROWS, SSD_WIDTH), lambda bi, s: (bi, s, 0)),
        out_shape=jax.ShapeDtypeStruct((b, t, SSD_WIDTH), F32),
        scratch_shapes=[pltpu.VMEM((SSD_GROUPS, SSD_STATE, GROUP_W), F32)],
        compiler_params=pltpu.CompilerParams(
            dimension_semantics=("parallel", "arbitrary"), vmem_limit_bytes=VMEM_LIMIT),
        name="ssd_forward",
    )(pf, pb, pf, consts["dt_bias"], consts["a_log"], consts["expand"])


def _attend(q_ref, rows, k_all, v_all, bias, sink_ref):
    left = lax.broadcasted_iota(jnp.int32, (CHUNK, LANES), 1) < HEAD_DIM
    n_pairs = ATTN_HEADS // 2
    scores = []
    for j in range(n_pairs):
        q2 = jnp.concatenate([q_ref[rows, 2 * j * LANES:(2 * j + 1) * LANES],
                              q_ref[rows, (2 * j + 1) * LANES:(2 * j + 2) * LANES]], axis=0)
        s = _dot_nt(q2, k_all)
        scores.append(s if bias is None else s + bias)
    n_keys = k_all.shape[0]
    probs, sink_terms = [], []
    for j, s in enumerate(scores):
        sk = jnp.concatenate([
            jnp.broadcast_to(sink_ref[2 * j:2 * j + 1, :], (CHUNK, LANES)),
            jnp.broadcast_to(sink_ref[2 * j + 1:2 * j + 2, :], (CHUNK, LANES))], axis=0)
        m = jnp.maximum(jnp.broadcast_to(jnp.max(s, axis=-1, keepdims=True), (2 * CHUNK, LANES)), sk)
        probs.append(jnp.concatenate(
            [jnp.exp(s[:, i * LANES:(i + 1) * LANES] - m).astype(BF16) for i in range(n_keys // LANES)],
            axis=1))
        sink_terms.append(jnp.exp(sk - m))
    v_ones = jnp.concatenate([v_all, jnp.ones((n_keys, LANES), BF16)], axis=1)
    outs = []
    for j in range(n_pairs):
        pv = _dot(probs[j], v_ones)
        o = pv[:, 0:LANES] / (pv[:, LANES:2 * LANES] + sink_terms[j])
        o0, o1 = o[0:CHUNK], o[CHUNK:2 * CHUNK]
        if j // 2 == 0:
            outs.append(jnp.where(left, o0, pltpu.roll(o1, HEAD_DIM, 1)))
        else:
            outs.append(jnp.where(left, pltpu.roll(o0, HEAD_DIM, 1), o1))
    return outs


def _mix_bwd_kernel(final, split, n_ctx_blocks, n_blocks, *refs):
    if split:
        ctx_ref, refs = refs[0], refs[1:]
    (xc_ref, bc_ref, dt_ref, zs_ref, yf_ref, yc_ref, ga_ref, q_ref, kv_ref, kvp_ref, kvn_ref,
     kvc_ref, x_ref, mod_ref, wraw_ref, dtb_ref, alog_ref, e_ref, dsk_ref, nw_ref, sink_ref,
     bias_ref) = refs[:22]
    if final:
        fw_ref, o_ref, state_ref, ycat_ref, w_ref = refs[22:]
    else:
        o_ref, state_ref, ycat_ref, w_ref = refs[22:]

    step = pl.program_id(1)

    @pl.when(jnp.logical_and(pl.program_id(0) == 0, step == 0))
    def _():
        for r0 in range(0, MIX_WIDTH, 512):
            w_ref[r0:r0 + 512, :] = wraw_ref[r0:r0 + 512, :].astype(BF16)

    blk = _scan_chunk(step, 1, n_ctx_blocks, n_blocks)
    per = SCAN_ROWS // CHUNK
    n_ctx_chunks, n_chunks = per * n_ctx_blocks, per * n_blocks

    @pl.when(step == 0)
    def _():
        state_ref[...] = jnp.zeros_like(state_ref)

    col = lax.broadcasted_iota(jnp.int32, (1, bias_ref.shape[1]), 1)
    for k in reversed(range(per)):
        rows = slice(k * CHUNK, (k + 1) * CHUNK)
        c = blk * per + k
        xc = xc_ref[rows, :]
        bc = bc_ref[rows, :]
        prep = _ssd_prepare(1, xc, dt_ref[rows, :], dtb_ref, alog_ref, e_ref)

        y = _ssd_output(prep, bc, state_ref) + yf_ref[rows, :] + xc * dsk_ref[...]
        gated = y * zs_ref[rows, :]
        for g in range(SSD_GROUPS):
            gg = gated[:, g * GROUP_W:(g + 1) * GROUP_W]
            ms = jnp.mean(gg * gg, axis=-1, keepdims=True)
            ycat_ref[rows, g * GROUP_W:(g + 1) * GROUP_W] = (
                gg * lax.rsqrt(ms + EPS) * nw_ref[:, g * GROUP_W:(g + 1) * GROUP_W]).astype(BF16)
        _ssd_update_state(prep, bc, state_ref)

        halves = [kvp_ref] + [kv_ref.at[i * CHUNK:(i + 1) * CHUNK, :] for i in range(per)] + [kvn_ref]
        win = halves[k:k + 3] + [kvc_ref]
        k_all = jnp.concatenate([r[:, 0:LANES] for r in win], axis=0)
        v_all = jnp.concatenate([r[:, LANES:2 * LANES] for r in win], axis=0)
        latent = c >= n_ctx_chunks
        no_prev = jnp.where(c > n_ctx_chunks, 0.0, NEG_INF)
        no_cur = jnp.where(latent, 0.0, NEG_INF)
        no_next = jnp.where(jnp.logical_and(latent, c < n_chunks - 1), 0.0, NEG_INF)
        edge = jnp.where(col < CHUNK, no_prev,
                         jnp.where(col < 2 * CHUNK, no_cur, jnp.where(col < 3 * CHUNK, no_next, 0.0)))
        slabs = _attend(q_ref, rows, k_all, v_all, bias_ref[...] + edge, sink_ref)
        ycat_ref[rows, SSD_WIDTH + SC_WIDTH:MIX_WIDTH] = (
            jnp.concatenate(slabs, axis=1) * ga_ref[rows, :]).astype(BF16)

    acc = _dot(ycat_ref[:, 0:SSD_WIDTH], w_ref[0:SSD_WIDTH, :])
    acc = acc + _dot(yc_ref[...], w_ref[SSD_WIDTH:SSD_WIDTH + SC_WIDTH, :])
    acc = acc + _dot(ycat_ref[:, SSD_WIDTH + SC_WIDTH:MIX_WIDTH], w_ref[SSD_WIDTH + SC_WIDTH:MIX_WIDTH, :])
    x_in = x_ref[...]
    if split:
        x_in = jnp.where(blk < n_ctx_blocks, ctx_ref[...], x_in)
    xn = x_in + mod_ref[2:3, :] * acc
    if final:
        xn = _rmsnorm(xn, fw_ref[...])
    o_ref[...] = xn


def _mix_backward(pf, pb, yf, xa, ctx, modsel, w_out_b, consts, n_ctx_blocks, final_norm_w=None):
    split = ctx is not None
    b, t, _ = pf.shape
    d = xa.shape[-1]
    n_blocks = t // SCAN_ROWS
    per = SCAN_ROWS // CHUNK
    n_chunks = per * n_blocks
    final = final_norm_w is not None
    n_ctx = n_ctx_blocks * SCAN_ROWS
    x_off = n_ctx_blocks if split else 0

    def bidx(s):
        return _scan_chunk(s, 1, n_ctx_blocks, n_blocks)

    def blk(width, offset):
        return pl.BlockSpec((None, SCAN_ROWS, width), lambda bi, s: (bi, bidx(s), offset // width))

    def const(shape):
        return pl.BlockSpec(shape, lambda bi, s: (0,) * len(shape))

    kvw = 2 * LANES
    in_specs = [
        blk(SSD_WIDTH, F_XC), blk(BC_WIDTH, B_BC), blk(LANES, F_DT), blk(SSD_WIDTH, F_ZS),
        blk(SSD_WIDTH, 0),
        blk(SC_WIDTH, B_YC), blk(ATTN_WIDTH, F_GA), blk(2 * ATTN_WIDTH, B_Q),
        blk(kvw, B_KV),
        pl.BlockSpec((None, CHUNK, kvw),
                     lambda bi, s: (bi, jnp.maximum(bidx(s) * per - 1, 0), B_KV // kvw)),
        pl.BlockSpec((None, CHUNK, kvw),
                     lambda bi, s: (bi, jnp.minimum((bidx(s) + 1) * per, n_chunks - 1), B_KV // kvw)),
        pl.BlockSpec((None, n_ctx, kvw), lambda bi, s: (bi, 0, B_KV // kvw)),
        pl.BlockSpec((None, SCAN_ROWS, d),
                     lambda bi, s: (bi, jnp.maximum(bidx(s) - x_off, 0), 0)),
        pl.BlockSpec((None, None, 3, d),
                     lambda bi, s: (bi, jnp.where(bidx(s) >= n_ctx_blocks, 1, 0), 0, 0)),
        pl.BlockSpec((MIX_WIDTH, d), lambda bi, s: (0, 0), pipeline_mode=pl.Buffered(1)),
        pl.BlockSpec((None, 1, LANES), lambda bi, s: (1, 0, 0)),
        pl.BlockSpec((None, 1, LANES), lambda bi, s: (1, 0, 0)),
        pl.BlockSpec((None, LANES, SSD_WIDTH), lambda bi, s: (1, 0, 0)),
        const((1, SSD_WIDTH)), const((1, SSD_WIDTH)), const((ATTN_HEADS, LANES)),
        const((2 * CHUNK, 3 * CHUNK + n_ctx)),
    ]
    args = [pf, pb, pf, pf, yf, pb, pf, pb, pb, pb, pb, pb, xa, modsel, w_out_b,
            consts["dt_bias"], consts["a_log"], consts["expand"], consts["d_skip"],
            consts["ssd_norm_w"], consts["sink"], consts["band_bias"]]
    if split:
        assert ctx.shape[1] == SCAN_ROWS and n_ctx_blocks == 1
        in_specs.insert(0, pl.BlockSpec((None, SCAN_ROWS, d), lambda bi, s: (bi, 0, 0)))
        args.insert(0, ctx)
    if final:
        in_specs.append(const((1, d)))
        args.append(final_norm_w.reshape(1, d))
        out_spec = pl.BlockSpec(
            (None, SCAN_ROWS, d),
            lambda bi, s: (bi, jnp.where(bidx(s) >= n_ctx_blocks, bidx(s), n_blocks - 1) - n_ctx_blocks, 0))
        out_shape = jax.ShapeDtypeStruct((b, t - n_ctx, d), F32)
    else:
        out_spec = blk(d, 0)
        out_shape = jax.ShapeDtypeStruct((b, t, d), F32)
    return pl.pallas_call(
        functools.partial(_mix_bwd_kernel, final, split, n_ctx_blocks, n_blocks),
        grid=(b, n_blocks),
        in_specs=in_specs,
        out_specs=out_spec,
        out_shape=out_shape,
        scratch_shapes=[pltpu.VMEM((SSD_GROUPS, SSD_STATE, GROUP_W), F32),
                        pltpu.VMEM((SCAN_ROWS, MIX_WIDTH), BF16),
                        pltpu.VMEM((MIX_WIDTH, d), BF16)],
        compiler_params=pltpu.CompilerParams(
            dimension_semantics=("arbitrary", "arbitrary"), vmem_limit_bytes=VMEM_LIMIT),
        name="mix_backward_final" if final else "mix_backward",
    )(*args)


def _rope_table(n_ctx, n_lat):
    n_rows = n_lat // GRID_W
    rows = np.repeat(np.arange(n_rows), GRID_W).astype(np.float32)
    cols = np.tile(np.arange(GRID_W), n_rows).astype(np.float32)
    axis_dim = HEAD_DIM // 2
    inv_freq = np.float32(ROPE_BASE) ** (-np.arange(0, axis_dim, 2, dtype=np.float32) / np.float32(axis_dim))
    ang = np.concatenate([rows[:, None] * inv_freq, cols[:, None] * inv_freq], axis=-1)
    cos, sin = np.cos(ang).astype(np.float32), np.sin(ang).astype(np.float32)
    q = HEAD_DIM // 4
    cos_h = np.concatenate([cos[:, 0:q], cos[:, 0:q], cos[:, q:2 * q], cos[:, q:2 * q]], axis=1)
    sin_h = np.concatenate([-sin[:, 0:q], sin[:, 0:q], -sin[:, q:2 * q], sin[:, q:2 * q]], axis=1)
    tab = np.concatenate([cos_h, cos_h, sin_h, sin_h], axis=1)
    ident = np.concatenate([np.ones((n_ctx, LANES), np.float32), np.zeros((n_ctx, LANES), np.float32)], axis=1)
    return jnp.asarray(np.concatenate([ident, tab], axis=0))


def _expand_matrix():
    e = np.zeros((2, LANES, SSD_WIDTH), np.float32)
    for d in range(2):
        for h in range(SSD_HEADS):
            e[d, d * SSD_HEADS + h, h * HEAD_DIM:(h + 1) * HEAD_DIM] = 1.0
    return jnp.asarray(e, BF16)


def _band_bias(n_ctx):
    row = np.arange(2 * CHUNK)[:, None] % CHUNK
    col = np.arange(3 * CHUNK + n_ctx)[None, :]
    ok = (col >= 3 * CHUNK) | ((col >= row) & (col <= row + 2 * CHUNK))
    return jnp.asarray(np.where(ok, 0.0, NEG_INF), F32)


def _dir_rows(v):
    out = jnp.zeros((2, 1, LANES), F32)
    for d in range(2):
        out = out.at[d, 0, d * SSD_HEADS:(d + 1) * SSD_HEADS].set(v[d])
    return out


def kernel(x, c, ctx, c_ctx, norm_w, w_mod, b_mod, w_in, ssd_conv_w, ssd_conv_b, ssd_dt_bias,
           ssd_a_log, ssd_d, ssd_norm_w, sc_conv_w, attn_sink, w_out, final_norm_w):
    b, n_lat, d = x.shape
    n_ctx = ctx.shape[1]
    depth = w_in.shape[0]
    assert d == D_MODEL and n_ctx % ROW_TILE == 0 and n_lat % ROW_TILE == 0 and b <= 7
    n_ctx_chunks = n_ctx // CHUNK
    n_ctx_tiles = n_ctx // ROW_TILE

    cvec = jnp.zeros((8, d), F32).at[0:b].set(c).at[b].set(c_ctx)
    mod = _modulation(cvec, w_mod, b_mod)
    mod = mod.reshape(depth, 8, 3, d)
    modsel = jnp.stack([jnp.broadcast_to(mod[:, b][:, None], (depth, b, 3, d)), mod[:, 0:b]], axis=2)

    xa, ctx_in = x, ctx
    assert n_ctx == ROW_TILE == SCAN_ROWS
    n_ctx_blocks = n_ctx // SCAN_ROWS
    rope_tab = _rope_table(n_ctx, n_lat)
    expand = _expand_matrix()
    band_bias = _band_bias(n_ctx)

    out = None
    for l in range(depth):
        last = l == depth - 1
        w_in_p = w_in[l]
        cw = ssd_conv_w[l]
        cbias = ssd_conv_b[l]
        consts = {
            "cw_x": cw[:, 0:SSD_WIDTH], "cb_x": cbias[0:SSD_WIDTH].reshape(1, -1),
            "cw_bc": cw[:, SSD_WIDTH:], "cb_bc": cbias[SSD_WIDTH:].reshape(1, -1),
            "cw_sc": sc_conv_w[l],
            "dt_bias": _dir_rows(ssd_dt_bias[l]), "a_log": _dir_rows(ssd_a_log[l]),
            "expand": expand,
            "d_skip": jnp.repeat(ssd_d[l], HEAD_DIM).reshape(1, -1),
            "ssd_norm_w": ssd_norm_w[l].reshape(1, -1),
            "sink": jnp.broadcast_to(attn_sink[l][:, None], (ATTN_HEADS, LANES)),
            "band_bias": band_bias,
        }
        pf, pb = _inproj(xa, ctx_in, modsel[l], norm_w[l], w_in_p, consts, rope_tab, n_ctx_tiles)
        yf = _ssd_forward(pf, pb, consts)
        w_out_b = w_out[l]
        if last:
            out = _mix_backward(pf, pb, yf, xa, ctx_in, modsel[l], w_out_b, consts, n_ctx_blocks,
                                final_norm_w)
        else:
            xa = _mix_backward(pf, pb, yf, xa, ctx_in, modsel[l], w_out_b, consts, n_ctx_blocks)
            ctx_in = None
    return out
```

```python
import functools

import numpy as np
import jax
import jax.numpy as jnp
from jax import lax
from jax.experimental import pallas as pl
from jax.experimental.pallas import tpu as pltpu

F32 = jnp.float32
BF16 = jnp.bfloat16

EPS = 1e-6
NEG_INF = -1e30
ROPE_BASE = 10000.0
GRID_W = 64

D_MODEL = 1024
CHUNK = 128
HALO = 8
ROW_TILE = 256
SSD_WIDTH = 1024
SSD_HEADS = 16
SSD_GROUPS = 2
SSD_STATE = 128
GROUP_W = SSD_WIDTH // SSD_GROUPS
BC_WIDTH = 2 * SSD_GROUPS * SSD_STATE
SC_WIDTH = 512
ATTN_WIDTH = 512
ATTN_HEADS = 8
ATTN_REP = 4
HEAD_DIM = 64
MIX_WIDTH = 2048
LANES = 128

W_X = 0
W_BC = 1024
W_VC = 1536
W_ZS = 2560
W_BZ = 3584
W_Q = 4608
W_ZA = 5120
W_KV = 5632
W_DT = 5888
PW = 6016
_W_GROUPS = ((1024, W_X, 1024), (2048, W_BC, 512), (2592, W_VC, 1024), (0, W_ZS, 1024),
             (3616, W_BZ, 1024), (4640, W_Q, 512), (5408, W_ZA, 512), (5152, W_KV, 256),
             (2560, W_DT, 32))
IN_COLS = 5920

F_XC = 0
F_ZS = 1024
F_GA = 2048
F_DT = 2560
FW = 2688
B_Q = 0
B_BC = 1024
B_YC = 1536
B_KV = 2048
BW = 2304
SCAN_ROWS = 2 * CHUNK

VMEM_LIMIT = 56 * 1024 * 1024


def _sigmoid(v):
    return 1.0 / (1.0 + jnp.exp(-v))


def _silu(v):
    return v * _sigmoid(v)


def _softplus(v):
    return jnp.maximum(v, 0.0) + jnp.log1p(jnp.exp(-jnp.abs(v)))


def _split2(v):
    hi = v.astype(BF16)
    lo = (v - hi.astype(F32)).astype(BF16)
    return hi, lo


def _split3(v):
    hi = v.astype(BF16)
    r = v - hi.astype(F32)
    mid = r.astype(BF16)
    lo = (r - mid.astype(F32)).astype(BF16)
    return hi, mid, lo


def _dot(a, b):
    return jnp.dot(a, b, preferred_element_type=F32)


def _dot_nt(a, b):
    return lax.dot_general(a, b, (((1,), (1,)), ((), ())), preferred_element_type=F32)


def _rmsnorm(x, w):
    ms = jnp.mean(x * x, axis=-1, keepdims=True)
    return x * lax.rsqrt(ms + EPS) * w


def _mod_kernel(c_ref, w_ref, b_ref, o_ref):
    sc = _silu(c_ref[...])
    o_ref[...] = _dot(sc.astype(BF16), w_ref[...].astype(BF16)) + b_ref[...]


def _modulation(cvec, w_mod, b_mod):
    depth = w_mod.shape[0]
    nblk = w_mod.shape[2] // D_MODEL
    return pl.pallas_call(
        _mod_kernel,
        grid=(depth, nblk),
        in_specs=[
            pl.BlockSpec((8, D_MODEL), lambda l, j: (0, 0)),
            pl.BlockSpec((None, D_MODEL, D_MODEL), lambda l, j: (l, 0, j)),
            pl.BlockSpec((None, 1, D_MODEL), lambda l, j: (l, 0, j)),
        ],
        out_specs=pl.BlockSpec((None, 8, D_MODEL), lambda l, j: (l, 0, j)),
        out_shape=jax.ShapeDtypeStruct((depth, 8, w_mod.shape[2]), F32),
        compiler_params=pltpu.CompilerParams(vmem_limit_bytes=VMEM_LIMIT),
        name="modulation",
    )(cvec, w_mod, b_mod.reshape(depth, 1, -1))


def _conv3(t, prev_row, next_row, w_ref):
    n = t.shape[0]
    rows = lax.broadcasted_iota(jnp.int32, t.shape, 0)
    tp = jnp.where(rows == 0, prev_row, pltpu.roll(t, 1, 0))
    tn = jnp.where(rows == n - 1, next_row, pltpu.roll(t, n - 1, 0))
    return w_ref[0:1, :] * tp + w_ref[1:2, :] * t + w_ref[2:3, :] * tn


def _rope(t, cos, sin_signed):
    lane = lax.broadcasted_iota(jnp.int32, t.shape, 1)
    first = (lane % 32) < 16
    swapped = jnp.where(first, pltpu.roll(t, LANES - 16, 1), pltpu.roll(t, 16, 1))
    return t * cos + swapped * sin_signed


def _inproj_kernel(split, n_ctx_tiles, n_tiles, *refs):
    if split:
        ctx_ref, refs = refs[0], refs[1:]
    (x_ref, xp_ref, xn_ref, mod_ref, nw_ref, wraw_ref, cwx_ref, cbx_ref, cwbc_ref,
     cbbc_ref, cwsc_ref, rope_ref, f_ref, b_ref, w_ref) = refs
    i = pl.program_id(1)

    @pl.when(jnp.logical_and(pl.program_id(0) == 0, i == 0))
    def _():
        for src, dst, width in _W_GROUPS:
            for c0 in range(0, width, 512):
                cw = min(512, width - c0)
                w_ref[:, dst + c0:dst + c0 + cw] = wraw_ref[:, src + c0:src + c0 + cw].astype(BF16)
        pad0 = W_DT + 2 * SSD_HEADS
        w_ref[:, pad0:PW] = jnp.zeros((D_MODEL, PW - pad0), BF16)

    has_prev = jnp.logical_and(i != 0, i != n_ctx_tiles).astype(F32)
    has_next = jnp.logical_and(i != n_ctx_tiles - 1, i != n_tiles - 1).astype(F32)

    def norm_mod(x):
        return _rmsnorm(x, nw_ref[...]) * (1.0 + mod_ref[1:2, :]) + mod_ref[0:1, :]

    x_tile = x_ref[...]
    if split:
        x_tile = jnp.where(i < n_ctx_tiles, ctx_ref[...], x_tile)
    h32 = norm_mod(x_tile)
    h = h32.astype(BF16)
    h_ext = jnp.concatenate([h32, norm_mod(xp_ref[...]), norm_mod(xn_ref[...])], axis=0).astype(BF16)
    prev_row, next_row = ROW_TILE + HALO - 1, ROW_TILE + HALO

    def proj(col, width):
        return _dot(h, w_ref[:, col:col + width])

    def proj_ext(col, width):
        pe = _dot(h_ext, w_ref[:, col:col + width])
        return pe[0:ROW_TILE], pe[prev_row:prev_row + 1], pe[next_row:next_row + 1]

    half = SSD_WIDTH // 2
    for k in range(2):
        p, pv, nx = proj_ext(W_X + k * half, half)
        y = _conv3(p, pv * has_prev, nx * has_next, cwx_ref.at[:, k * half:(k + 1) * half])
        f_ref[:, F_XC + k * half:F_XC + (k + 1) * half] = _silu(y + cbx_ref[:, k * half:(k + 1) * half])
    p, pv, nx = proj_ext(W_BC, BC_WIDTH)
    y = _conv3(p, pv * has_prev, nx * has_next, cwbc_ref) + cbbc_ref[...]
    b_ref[:, B_BC:B_BC + BC_WIDTH] = _silu(y).astype(BF16)

    w = SC_WIDTH
    vc, vcp, vcn = proj_ext(W_VC, 2 * w)
    conv = _conv3(vc[:, 0:w] * vc[:, w:2 * w], vcp[:, 0:w] * vcp[:, w:2 * w] * has_prev,
                  vcn[:, 0:w] * vcn[:, w:2 * w] * has_next, cwsc_ref)
    bz = proj(W_BZ, 2 * w)
    b_ref[:, B_YC:B_YC + w] = (bz[:, 0:w] * conv * _silu(bz[:, w:2 * w])).astype(BF16)

    for k in range(2):
        f_ref[:, F_ZS + k * half:F_ZS + (k + 1) * half] = _silu(proj(W_ZS + k * half, half))
    f_ref[:, F_GA:F_GA + ATTN_WIDTH] = _silu(proj(W_ZA, ATTN_WIDTH))
    f_ref[:, F_DT:F_DT + LANES] = proj(W_DT, LANES)

    cos, sin = rope_ref[:, 0:LANES], rope_ref[:, LANES:2 * LANES]
    left = lax.broadcasted_iota(jnp.int32, (ROW_TILE, LANES), 1) < HEAD_DIM
    q = proj(W_Q, ATTN_WIDTH)
    scale = HEAD_DIM ** -0.5
    for j in range(ATTN_WIDTH // LANES):
        qs = _rope(q[:, j * LANES:(j + 1) * LANES], cos, sin) * scale
        qr = pltpu.roll(qs, HEAD_DIM, 1)
        if j // 2 == 0:
            h0, h1 = jnp.where(left, qs, 0.0), jnp.where(left, qr, 0.0)
        else:
            h0, h1 = jnp.where(left, 0.0, qr), jnp.where(left, 0.0, qs)
        b_ref[:, B_Q + 2 * j * LANES:B_Q + (2 * j + 1) * LANES] = h0.astype(BF16)
        b_ref[:, B_Q + (2 * j + 1) * LANES:B_Q + (2 * j + 2) * LANES] = h1.astype(BF16)
    kv = proj(W_KV, 2 * LANES)
    b_ref[:, B_KV:B_KV + LANES] = _rope(kv[:, 0:LANES], cos, sin).astype(BF16)
    b_ref[:, B_KV + LANES:B_KV + 2 * LANES] = kv[:, LANES:2 * LANES].astype(BF16)


def _inproj(xa, ctx, modsel, norm_w, w_in, layer, consts, rope_tab, n_ctx_tiles):
    split = ctx is not None
    b, rows, d = xa.shape
    off = n_ctx_tiles if split else 0
    t = rows + off * ROW_TILE
    n_tiles = t // ROW_TILE
    per = ROW_TILE // HALO
    n_halo = rows // HALO

    def const(shape):
        return pl.BlockSpec(shape, lambda bi, i: (0,) * len(shape))

    in_specs = [
            pl.BlockSpec((None, ROW_TILE, d), lambda bi, i: (bi, jnp.maximum(i - off, 0), 0)),
            pl.BlockSpec((None, HALO, d),
                         lambda bi, i: (bi, jnp.maximum((i - off) * per - 1, 0), 0)),
            pl.BlockSpec((None, HALO, d),
                         lambda bi, i: (bi, jnp.clip((i - off + 1) * per, 0, n_halo - 1), 0)),
            pl.BlockSpec((None, None, 3, d),
                         lambda bi, i: (bi, jnp.where(i >= n_ctx_tiles, 1, 0), 0, 0)),
            const((1, d)),
            pl.BlockSpec((None, d, IN_COLS), lambda bi, i: (layer, 0, 0), pipeline_mode=pl.Buffered(1)),
            const((3, SSD_WIDTH)), const((1, SSD_WIDTH)), const((3, BC_WIDTH)), const((1, BC_WIDTH)),
            const((3, SC_WIDTH)),
            pl.BlockSpec((ROW_TILE, 2 * LANES), lambda bi, i: (i, 0)),
    ]
    args = [xa, xa, xa, modsel, norm_w.reshape(1, d), w_in, consts["cw_x"], consts["cb_x"],
            consts["cw_bc"], consts["cb_bc"], consts["cw_sc"], rope_tab]
    if split:
        assert ctx.shape[1] == ROW_TILE and n_ctx_tiles == 1
        in_specs.insert(0, pl.BlockSpec((None, ROW_TILE, d), lambda bi, i: (bi, 0, 0)))
        args.insert(0, ctx)
    return pl.pallas_call(
        functools.partial(_inproj_kernel, split, n_ctx_tiles, n_tiles),
        grid=(b, n_tiles),
        in_specs=in_specs,
        out_specs=[
            pl.BlockSpec((None, ROW_TILE, FW), lambda bi, i: (bi, i, 0)),
            pl.BlockSpec((None, ROW_TILE, BW), lambda bi, i: (bi, i, 0)),
        ],
        out_shape=[jax.ShapeDtypeStruct((b, t, FW), F32), jax.ShapeDtypeStruct((b, t, BW), BF16)],
        scratch_shapes=[pltpu.VMEM((d, PW), BF16)],
        compiler_params=pltpu.CompilerParams(
            dimension_semantics=("arbitrary", "arbitrary"), vmem_limit_bytes=VMEM_LIMIT),
        name="in_projection",
    )(*args)


def _scan_chunk(step, direction, n_ctx_chunks, n_chunks):
    if direction == 0:
        return step
    return jnp.where(step < n_ctx_chunks, n_ctx_chunks - 1 - step,
                     n_chunks - 1 - (step - n_ctx_chunks))


def _ssd_prepare(direction, xc, dt_raw, dtb_ref, alog_ref, e_ref):
    lane = lax.broadcasted_iota(jnp.int32, (1, LANES), 1)
    mine = jnp.logical_and(lane >= SSD_HEADS * direction, lane < SSD_HEADS * (direction + 1))
    a_row = jnp.where(mine, -jnp.exp(alog_ref[...]), 0.0)
    dt = _softplus(dt_raw + dtb_ref[...])
    da = dt * a_row

    r_i = lax.broadcasted_iota(jnp.int32, (CHUNK, CHUNK), 0)
    c_i = lax.broadcasted_iota(jnp.int32, (CHUNK, CHUNK), 1)
    tri = (c_i <= r_i) if direction == 0 else (c_i >= r_i)
    tri_b = jnp.where(tri, 1.0, 0.0).astype(BF16)
    u3 = _dot(tri_b, jnp.concatenate(_split3(da), axis=1))
    u = u3[:, 0:LANES] + u3[:, LANES:2 * LANES] + u3[:, 2 * LANES:3 * LANES]
    tot_row = CHUNK - 1 if direction == 0 else 0
    tot = u[tot_row:tot_row + 1, :]
    exp_u = jnp.exp(u)
    exp_d = jnp.exp(tot - u)

    stack = jnp.concatenate((dt.astype(BF16), exp_d.astype(BF16)) + _split2(exp_u), axis=0)
    ex = _dot(stack, e_ref[...])
    xdt = xc * ex[0:CHUNK]
    xdec = xdt * ex[CHUNK:2 * CHUNK]
    eu_e = ex[2 * CHUNK:3 * CHUNK] + ex[3 * CHUNK:4 * CHUNK]
    return dict(direction=direction, tri=tri, u=u, u_t=u.T, xdt=xdt, xdec=xdec, eu_e=eu_e,
                tot_row=tot_row)


def _ssd_output(prep, bc, state_ref):
    direction, tri, u, u_t = prep["direction"], prep["tri"], prep["u"], prep["u_t"]
    xdt, eu_e = prep["xdt"], prep["eu_e"]
    half = lax.broadcasted_iota(jnp.int32, (CHUNK, LANES), 1) < HEAD_DIM
    y_slabs = []
    for g in range(SSD_GROUPS):
        b_g = bc[:, g * SSD_STATE:(g + 1) * SSD_STATE]
        c_g = bc[:, (SSD_GROUPS + g) * SSD_STATE:(SSD_GROUPS + g + 1) * SSD_STATE]
        cb = _dot_nt(c_g, b_g)
        y_off = _dot(c_g, state_ref[g].astype(BF16)) * eu_e[:, g * GROUP_W:(g + 1) * GROUP_W]
        for j in range(GROUP_W // LANES):
            ms = []
            for hh in range(2):
                hl = SSD_HEADS * direction + g * (SSD_HEADS // SSD_GROUPS) + 2 * j + hh
                diff = u[:, hl:hl + 1] - u_t[hl:hl + 1, :]
                ms.append(cb * jnp.exp(jnp.where(tri, diff, -jnp.inf)))
            m2 = jnp.concatenate(ms, axis=1).astype(BF16)
            lo = g * GROUP_W + j * LANES
            slab = xdt[:, lo:lo + LANES]
            xbd = jnp.concatenate([jnp.where(half, slab, 0.0), jnp.where(half, 0.0, slab)],
                                  axis=0).astype(BF16)
            y_slabs.append(_dot(m2, xbd) + y_off[:, j * LANES:(j + 1) * LANES])
    return jnp.concatenate(y_slabs, axis=1)


def _ssd_update_state(prep, bc, state_ref):
    eu_e, xdec, tot_row = prep["eu_e"], prep["xdec"], prep["tot_row"]
    for g in range(SSD_GROUPS):
        b_g = bc[:, g * SSD_STATE:(g + 1) * SSD_STATE]
        decay = eu_e[tot_row:tot_row + 1, g * GROUP_W:(g + 1) * GROUP_W]
        state_ref[g] = state_ref[g] * decay + _dot(
            b_g.astype(F32).T.astype(BF16), xdec[:, g * GROUP_W:(g + 1) * GROUP_W].astype(BF16))


def _ssd_fwd_kernel(xc_ref, bc_ref, dt_ref, dtb_ref, alog_ref, e_ref, o_ref, state_ref):
    @pl.when(pl.program_id(1) == 0)
    def _():
        state_ref[...] = jnp.zeros_like(state_ref)

    for k in range(SCAN_ROWS // CHUNK):
        rows = slice(k * CHUNK, (k + 1) * CHUNK)
        prep = _ssd_prepare(0, xc_ref[rows, :], dt_ref[rows, :], dtb_ref, alog_ref, e_ref)
        bc = bc_ref[rows, :]
        o_ref[rows, :] = _ssd_output(prep, bc, state_ref)
        _ssd_update_state(prep, bc, state_ref)


def _ssd_forward(pf, pb, consts):
    b, t, _ = pf.shape
    return pl.pallas_call(
        _ssd_fwd_kernel,
        grid=(b, t // SCAN_ROWS),
        in_specs=[
            pl.BlockSpec((None, SCAN_ROWS, SSD_WIDTH), lambda bi, s: (bi, s, F_XC // SSD_WIDTH)),
            pl.BlockSpec((None, SCAN_ROWS, BC_WIDTH), lambda bi, s: (bi, s, B_BC // BC_WIDTH)),
            pl.BlockSpec((None, SCAN_ROWS, LANES), lambda bi, s: (bi, s, F_DT // LANES)),
            pl.BlockSpec((None, 1, LANES), lambda bi, s: (0, 0, 0)),
            pl.BlockSpec((None, 1, LANES), lambda bi, s: (0, 0, 0)),
            pl.BlockSpec((None, LANES, SSD_WIDTH), lambda bi, s: (0, 0, 0)),
        ],
        out_specs=pl.BlockSpec((None, SCAN_ROWS, SSD_WIDTH), lambda bi, s: (bi, s, 0)),
        out_shape=jax.ShapeDtypeStruct((b, t, SSD_WIDTH), F32),
        scratch_shapes=[pltpu.VMEM((SSD_GROUPS, SSD_STATE, GROUP_W), F32)],
        compiler_params=pltpu.CompilerParams(
            dimension_semantics=("parallel", "arbitrary"), vmem_limit_bytes=VMEM_LIMIT),
        name="ssd_forward",
    )(pf, pb, pf, consts["dt_bias"], consts["a_log"], consts["expand"])


def _attn_scores(q_ref, rows, k_all, bias):
    scores = []
    for j in range(ATTN_HEADS // 2):
        q2 = jnp.concatenate([q_ref[rows, 2 * j * LANES:(2 * j + 1) * LANES],
                              q_ref[rows, (2 * j + 1) * LANES:(2 * j + 2) * LANES]], axis=0)
        scores.append(_dot_nt(q2, k_all) + bias)
    return scores


def _attn_probs(scores, sink_ref):
    n_keys = scores[0].shape[1]
    probs, sink_terms = [], []
    for j, s in enumerate(scores):
        sk = jnp.concatenate([
            jnp.broadcast_to(sink_ref[2 * j:2 * j + 1, :], (CHUNK, LANES)),
            jnp.broadcast_to(sink_ref[2 * j + 1:2 * j + 2, :], (CHUNK, LANES))], axis=0)
        m = jnp.maximum(jnp.broadcast_to(jnp.max(s, axis=-1, keepdims=True), (2 * CHUNK, LANES)), sk)
        probs.append(jnp.concatenate(
            [jnp.exp(s[:, i * LANES:(i + 1) * LANES] - m).astype(BF16) for i in range(n_keys // LANES)],
            axis=1))
        sink_terms.append(jnp.exp(sk - m))
    return probs, sink_terms


def _attn_values(probs, sink_terms, v_all):
    left = lax.broadcasted_iota(jnp.int32, (CHUNK, LANES), 1) < HEAD_DIM
    v_ones = jnp.concatenate([v_all, jnp.ones((v_all.shape[0], LANES), BF16)], axis=1)
    outs = []
    for j in range(ATTN_HEADS // 2):
        pv = _dot(probs[j], v_ones)
        o = pv[:, 0:LANES] / (pv[:, LANES:2 * LANES] + sink_terms[j])
        o0, o1 = o[0:CHUNK], o[CHUNK:2 * CHUNK]
        if j // 2 == 0:
            outs.append(jnp.where(left, o0, pltpu.roll(o1, HEAD_DIM, 1)))
        else:
            outs.append(jnp.where(left, pltpu.roll(o0, HEAD_DIM, 1), o1))
    return outs


def _mix_bwd_kernel(final, split, n_ctx_blocks, n_blocks, *refs):
    if split:
        ctx_ref, refs = refs[0], refs[1:]
    (xc_ref, bc_ref, dt_ref, zs_ref, yf_ref, yc_ref, ga_ref, q_ref, kv_ref, kvp_ref, kvn_ref,
     kvc_ref, x_ref, mod_ref, wraw_ref, dtb_ref, alog_ref, e_ref, dsk_ref, nw_ref, sink_ref,
     bias_ref) = refs[:22]
    if final:
        fw_ref, o_ref, state_ref, ycat_ref, w_ref = refs[22:]
    else:
        o_ref, state_ref, ycat_ref, w_ref = refs[22:]

    step = pl.program_id(1)

    @pl.when(jnp.logical_and(pl.program_id(0) == 0, step == 0))
    def _():
        for r0 in range(0, MIX_WIDTH, 512):
            w_ref[r0:r0 + 512, :] = wraw_ref[r0:r0 + 512, :].astype(BF16)

    blk = _scan_chunk(step, 1, n_ctx_blocks, n_blocks)
    per = SCAN_ROWS // CHUNK
    n_ctx_chunks, n_chunks = per * n_ctx_blocks, per * n_blocks

    @pl.when(step == 0)
    def _():
        state_ref[...] = jnp.zeros_like(state_ref)

    col = lax.broadcasted_iota(jnp.int32, (1, bias_ref.shape[1]), 1)
    halves = [kvp_ref] + [kv_ref.at[i * CHUNK:(i + 1) * CHUNK, :] for i in range(per)] + [kvn_ref]
    for k in reversed(range(per)):
        rows = slice(k * CHUNK, (k + 1) * CHUNK)
        c = blk * per + k
        xc = xc_ref[rows, :]
        bc = bc_ref[rows, :]
        prep = _ssd_prepare(1, xc, dt_ref[rows, :], dtb_ref, alog_ref, e_ref)

        y = _ssd_output(prep, bc, state_ref) + yf_ref[rows, :] + xc * dsk_ref[...]
        gated = y * zs_ref[rows, :]
        for g in range(SSD_GROUPS):
            gg = gated[:, g * GROUP_W:(g + 1) * GROUP_W]
            ms = jnp.mean(gg * gg, axis=-1, keepdims=True)
            ycat_ref[rows, g * GROUP_W:(g + 1) * GROUP_W] = (
                gg * lax.rsqrt(ms + EPS) * nw_ref[:, g * GROUP_W:(g + 1) * GROUP_W]).astype(BF16)
        _ssd_update_state(prep, bc, state_ref)

        win = halves[k:k + 3] + [kvc_ref]
        k_all = jnp.concatenate([r[:, 0:LANES] for r in win], axis=0)
        v_all = jnp.concatenate([r[:, LANES:2 * LANES] for r in win], axis=0)
        latent = c >= n_ctx_chunks
        no_prev = jnp.where(c > n_ctx_chunks, 0.0, NEG_INF)
        no_cur = jnp.where(latent, 0.0, NEG_INF)
        no_next = jnp.where(jnp.logical_and(latent, c < n_chunks - 1), 0.0, NEG_INF)
        edge = jnp.where(col < CHUNK, no_prev,
                         jnp.where(col < 2 * CHUNK, no_cur, jnp.where(col < 3 * CHUNK, no_next, 0.0)))
        scores = _attn_scores(q_ref, rows, k_all, bias_ref[...] + edge)
        probs, sink_terms = _attn_probs(scores, sink_ref)
        slabs = _attn_values(probs, sink_terms, v_all)
        ycat_ref[rows, SSD_WIDTH + SC_WIDTH:MIX_WIDTH] = (
            jnp.concatenate(slabs, axis=1) * ga_ref[rows, :]).astype(BF16)

    acc = _dot(ycat_ref[:, 0:SSD_WIDTH], w_ref[0:SSD_WIDTH, :])
    acc = acc + _dot(yc_ref[...], w_ref[SSD_WIDTH:SSD_WIDTH + SC_WIDTH, :])
    acc = acc + _dot(ycat_ref[:, SSD_WIDTH + SC_WIDTH:MIX_WIDTH], w_ref[SSD_WIDTH + SC_WIDTH:MIX_WIDTH, :])
    x_in = x_ref[...]
    if split:
        x_in = jnp.where(blk < n_ctx_blocks, ctx_ref[...], x_in)
    xn = x_in + mod_ref[2:3, :] * acc
    if final:
        xn = _rmsnorm(xn, fw_ref[...])
    o_ref[...] = xn


def _mix_backward(pf, pb, yf, xa, ctx, modsel, w_out, layer, consts, n_ctx_blocks, final_norm_w=None):
    split = ctx is not None
    b, t, _ = pf.shape
    d = xa.shape[-1]
    n_blocks = t // SCAN_ROWS
    per = SCAN_ROWS // CHUNK
    n_chunks = per * n_blocks
    final = final_norm_w is not None
    n_ctx = n_ctx_blocks * SCAN_ROWS
    x_off = n_ctx_blocks if split else 0

    def bidx(s):
        return _scan_chunk(s, 1, n_ctx_blocks, n_blocks)

    def blk(width, offset):
        return pl.BlockSpec((None, SCAN_ROWS, width), lambda bi, s: (bi, bidx(s), offset // width))

    def const(shape):
        return pl.BlockSpec(shape, lambda bi, s: (0,) * len(shape))

    kvw = 2 * LANES
    in_specs = [
        blk(SSD_WIDTH, F_XC), blk(BC_WIDTH, B_BC), blk(LANES, F_DT), blk(SSD_WIDTH, F_ZS),
        blk(SSD_WIDTH, 0),
        blk(SC_WIDTH, B_YC), blk(ATTN_WIDTH, F_GA), blk(2 * ATTN_WIDTH, B_Q),
        blk(kvw, B_KV),
        pl.BlockSpec((None, CHUNK, kvw),
                     lambda bi, s: (bi, jnp.maximum(bidx(s) * per - 1, 0), B_KV // kvw)),
        pl.BlockSpec((None, CHUNK, kvw),
                     lambda bi, s: (bi, jnp.minimum((bidx(s) + 1) * per, n_chunks - 1), B_KV // kvw)),
        pl.BlockSpec((None, n_ctx, kvw), lambda bi, s: (bi, 0, B_KV // kvw)),
        pl.BlockSpec((None, SCAN_ROWS, d),
                     lambda bi, s: (bi, jnp.maximum(bidx(s) - x_off, 0), 0)),
        pl.BlockSpec((None, None, 3, d),
                     lambda bi, s: (bi, jnp.where(bidx(s) >= n_ctx_blocks, 1, 0), 0, 0)),
        pl.BlockSpec((None, MIX_WIDTH, d), lambda bi, s: (layer, 0, 0), pipeline_mode=pl.Buffered(1)),
        pl.BlockSpec((None, 1, LANES), lambda bi, s: (1, 0, 0)),
        pl.BlockSpec((None, 1, LANES), lambda bi, s: (1, 0, 0)),
        pl.BlockSpec((None, LANES, SSD_WIDTH), lambda bi, s: (1, 0, 0)),
        const((1, SSD_WIDTH)), const((1, SSD_WIDTH)), const((ATTN_HEADS, LANES)),
        const((2 * CHUNK, 3 * CHUNK + n_ctx)),
    ]
    args = [pf, pb, pf, pf, yf, pb, pf, pb, pb, pb, pb, pb, xa, modsel, w_out,
            consts["dt_bias"], consts["a_log"], consts["expand"], consts["d_skip"],
            consts["ssd_norm_w"], consts["sink"], consts["band_bias"]]
    if split:
        assert ctx.shape[1] == SCAN_ROWS and n_ctx_blocks == 1
        in_specs.insert(0, pl.BlockSpec((None, SCAN_ROWS, d), lambda bi, s: (bi, 0, 0)))
        args.insert(0, ctx)
    if final:
        in_specs.append(const((1, d)))
        args.append(final_norm_w.reshape(1, d))
        out_spec = pl.BlockSpec(
            (None, SCAN_ROWS, d),
            lambda bi, s: (bi, jnp.where(bidx(s) >= n_ctx_blocks, bidx(s), n_blocks - 1) - n_ctx_blocks, 0))
        out_shape = jax.ShapeDtypeStruct((b, t - n_ctx, d), F32)
    else:
        out_spec = blk(d, 0)
        out_shape = jax.ShapeDtypeStruct((b, t, d), F32)
    return pl.pallas_call(
        functools.partial(_mix_bwd_kernel, final, split, n_ctx_blocks, n_blocks),
        grid=(b, n_blocks),
        in_specs=in_specs,
        out_specs=out_spec,
        out_shape=out_shape,
        scratch_shapes=[pltpu.VMEM((SSD_GROUPS, SSD_STATE, GROUP_W), F32),
                        pltpu.VMEM((SCAN_ROWS, MIX_WIDTH), BF16),
                        pltpu.VMEM((MIX_WIDTH, d), BF16)],
        compiler_params=pltpu.CompilerParams(
            dimension_semantics=("arbitrary", "arbitrary"), vmem_limit_bytes=VMEM_LIMIT),
        name="mix_backward_final" if final else "mix_backward",
    )(*args)


def _rope_table(n_ctx, n_lat):
    n_rows = n_lat // GRID_W
    rows = np.repeat(np.arange(n_rows), GRID_W).astype(np.float32)
    cols = np.tile(np.arange(GRID_W), n_rows).astype(np.float32)
    axis_dim = HEAD_DIM // 2
    inv_freq = np.float32(ROPE_BASE) ** (-np.arange(0, axis_dim, 2, dtype=np.float32) / np.float32(axis_dim))
    ang = np.concatenate([rows[:, None] * inv_freq, cols[:, None] * inv_freq], axis=-1)
    cos, sin = np.cos(ang).astype(np.float32), np.sin(ang).astype(np.float32)
    q = HEAD_DIM // 4
    cos_h = np.concatenate([cos[:, 0:q], cos[:, 0:q], cos[:, q:2 * q], cos[:, q:2 * q]], axis=1)
    sin_h = np.concatenate([-sin[:, 0:q], sin[:, 0:q], -sin[:, q:2 * q], sin[:, q:2 * q]], axis=1)
    tab = np.concatenate([cos_h, cos_h, sin_h, sin_h], axis=1)
    ident = np.concatenate([np.ones((n_ctx, LANES), np.float32), np.zeros((n_ctx, LANES), np.float32)], axis=1)
    return jnp.asarray(np.concatenate([ident, tab], axis=0))


def _expand_matrix():
    e = np.zeros((2, LANES, SSD_WIDTH), np.float32)
    for d in range(2):
        for h in range(SSD_HEADS):
            e[d, d * SSD_HEADS + h, h * HEAD_DIM:(h + 1) * HEAD_DIM] = 1.0
    return jnp.asarray(e, BF16)


def _band_bias(n_ctx):
    row = np.arange(2 * CHUNK)[:, None] % CHUNK
    col = np.arange(3 * CHUNK + n_ctx)[None, :]
    ok = (col >= 3 * CHUNK) | ((col >= row) & (col <= row + 2 * CHUNK))
    return jnp.asarray(np.where(ok, 0.0, NEG_INF), F32)


def _dir_rows(v):
    out = jnp.zeros((2, 1, LANES), F32)
    for d in range(2):
        out = out.at[d, 0, d * SSD_HEADS:(d + 1) * SSD_HEADS].set(v[d])
    return out


def kernel(x, c, ctx, c_ctx, norm_w, w_mod, b_mod, w_in, ssd_conv_w, ssd_conv_b, ssd_dt_bias,
           ssd_a_log, ssd_d, ssd_norm_w, sc_conv_w, attn_sink, w_out, final_norm_w):
    b, n_lat, d = x.shape
    n_ctx = ctx.shape[1]
    depth = w_in.shape[0]
    assert d == D_MODEL and n_ctx % ROW_TILE == 0 and n_lat % ROW_TILE == 0 and b <= 7
    n_ctx_chunks = n_ctx // CHUNK
    n_ctx_tiles = n_ctx // ROW_TILE

    cvec = jnp.zeros((8, d), F32).at[0:b].set(c).at[b].set(c_ctx)
    mod = _modulation(cvec, w_mod, b_mod)
    mod = mod.reshape(depth, 8, 3, d)
    modsel = jnp.stack([jnp.broadcast_to(mod[:, b][:, None], (depth, b, 3, d)), mod[:, 0:b]], axis=2)

    xa, ctx_in = x, ctx
    assert n_ctx == ROW_TILE == SCAN_ROWS
    n_ctx_blocks = n_ctx // SCAN_ROWS
    rope_tab = _rope_table(n_ctx, n_lat)
    expand = _expand_matrix()
    band_bias = _band_bias(n_ctx)

    out = None
    for l in range(depth):
        last = l == depth - 1
        cw = ssd_conv_w[l]
        cbias = ssd_conv_b[l]
        consts = {
            "cw_x": cw[:, 0:SSD_WIDTH], "cb_x": cbias[0:SSD_WIDTH].reshape(1, -1),
            "cw_bc": cw[:, SSD_WIDTH:], "cb_bc": cbias[SSD_WIDTH:].reshape(1, -1),
            "cw_sc": sc_conv_w[l],
            "dt_bias": _dir_rows(ssd_dt_bias[l]), "a_log": _dir_rows(ssd_a_log[l]),
            "expand": expand,
            "d_skip": jnp.repeat(ssd_d[l], HEAD_DIM).reshape(1, -1),
            "ssd_norm_w": ssd_norm_w[l].reshape(1, -1),
            "sink": jnp.broadcast_to(attn_sink[l][:, None], (ATTN_HEADS, LANES)),
            "band_bias": band_bias,
        }
        pf, pb = _inproj(xa, ctx_in, modsel[l], norm_w[l], w_in, l, consts, rope_tab, n_ctx_tiles)
        yf = _ssd_forward(pf, pb, consts)
        if last:
            out = _mix_backward(pf, pb, yf, xa, ctx_in, modsel[l], w_out, l, consts, n_ctx_blocks,
                                final_norm_w)
        else:
            xa = _mix_backward(pf, pb, yf, xa, ctx_in, modsel[l], w_out, l, consts, n_ctx_blocks)
            ctx_in = None
    return out
```

```python
import functools

import numpy as np
import jax
import jax.numpy as jnp
from jax import lax
from jax.experimental import pallas as pl
from jax.experimental.pallas import tpu as pltpu

F32 = jnp.float32
BF16 = jnp.bfloat16

EPS = 1e-6
NEG_INF = -1e30
ROPE_BASE = 10000.0
GRID_W = 64

D_MODEL = 1024
CHUNK = 128
HALO = 8
ROW_TILE = 256
SSD_WIDTH = 1024
SSD_HEADS = 16
SSD_GROUPS = 2
SSD_STATE = 128
GROUP_W = SSD_WIDTH // SSD_GROUPS
BC_WIDTH = 2 * SSD_GROUPS * SSD_STATE
SC_WIDTH = 512
ATTN_WIDTH = 512
ATTN_HEADS = 8
ATTN_REP = 4
HEAD_DIM = 64
MIX_WIDTH = 2048
LANES = 128

W_X = 0
W_BC = 1024
W_VC = 1536
W_ZS = 2560
W_BZ = 3584
W_Q = 4608
W_ZA = 5120
W_KV = 5632
W_DT = 5888
PW = 6016
_W_GROUPS = ((1024, W_X, 1024), (2048, W_BC, 512), (2592, W_VC, 1024), (0, W_ZS, 1024),
             (3616, W_BZ, 1024), (4640, W_Q, 512), (5408, W_ZA, 512), (5152, W_KV, 256),
             (2560, W_DT, 32))
IN_COLS = 5920

F_XC = 0
F_ZS = 1024
F_GA = 2048
F_DT = 2560
FW = 2688
B_Q = 0
B_BC = 1024
B_YC = 1536
B_KV = 2048
BW = 2304
SCAN_ROWS = 2 * CHUNK

VMEM_LIMIT = 56 * 1024 * 1024


def _sigmoid(v):
    return 1.0 / (1.0 + jnp.exp(-v))


def _silu(v):
    return v * _sigmoid(v)


def _softplus(v):
    return jnp.maximum(v, 0.0) + jnp.log1p(jnp.exp(-jnp.abs(v)))


def _split2(v):
    hi = v.astype(BF16)
    lo = (v - hi.astype(F32)).astype(BF16)
    return hi, lo


def _split3(v):
    hi = v.astype(BF16)
    r = v - hi.astype(F32)
    mid = r.astype(BF16)
    lo = (r - mid.astype(F32)).astype(BF16)
    return hi, mid, lo


def _dot(a, b):
    return jnp.dot(a, b, preferred_element_type=F32)


def _dot_nt(a, b):
    return lax.dot_general(a, b, (((1,), (1,)), ((), ())), preferred_element_type=F32)


def _rmsnorm(x, w):
    ms = jnp.mean(x * x, axis=-1, keepdims=True)
    return x * lax.rsqrt(ms + EPS) * w


def _mod_kernel(c_ref, w_ref, b_ref, o_ref):
    sc = _silu(c_ref[...])
    o_ref[...] = _dot(sc.astype(BF16), w_ref[...].astype(BF16)) + b_ref[...]


def _modulation(cvec, w_mod, b_mod):
    depth = w_mod.shape[0]
    nblk = w_mod.shape[2] // D_MODEL
    return pl.pallas_call(
        _mod_kernel,
        grid=(depth, nblk),
        in_specs=[
            pl.BlockSpec((8, D_MODEL), lambda l, j: (0, 0)),
            pl.BlockSpec((None, D_MODEL, D_MODEL), lambda l, j: (l, 0, j)),
            pl.BlockSpec((None, 1, D_MODEL), lambda l, j: (l, 0, j)),
        ],
        out_specs=pl.BlockSpec((None, 8, D_MODEL), lambda l, j: (l, 0, j)),
        out_shape=jax.ShapeDtypeStruct((depth, 8, w_mod.shape[2]), F32),
        compiler_params=pltpu.CompilerParams(vmem_limit_bytes=VMEM_LIMIT),
        name="modulation",
    )(cvec, w_mod, b_mod.reshape(depth, 1, -1))


def _conv3(t, prev_row, next_row, w_ref):
    n = t.shape[0]
    rows = lax.broadcasted_iota(jnp.int32, t.shape, 0)
    tp = jnp.where(rows == 0, prev_row, pltpu.roll(t, 1, 0))
    tn = jnp.where(rows == n - 1, next_row, pltpu.roll(t, n - 1, 0))
    return w_ref[0:1, :] * tp + w_ref[1:2, :] * t + w_ref[2:3, :] * tn


def _rope(t, cos, sin_signed):
    lane = lax.broadcasted_iota(jnp.int32, t.shape, 1)
    first = (lane % 32) < 16
    swapped = jnp.where(first, pltpu.roll(t, LANES - 16, 1), pltpu.roll(t, 16, 1))
    return t * cos + swapped * sin_signed


def _inproj_kernel(split, n_ctx_tiles, n_tiles, *refs):
    if split:
        ctx_ref, refs = refs[0], refs[1:]
    (x_ref, xp_ref, xn_ref, mod_ref, nw_ref, wraw_ref, cwx_ref, cbx_ref, cwbc_ref,
     cbbc_ref, cwsc_ref, rope_ref, f_ref, b_ref, w_ref) = refs
    i = pl.program_id(1)

    @pl.when(jnp.logical_and(pl.program_id(0) == 0, i == 0))
    def _():
        for src, dst, width in _W_GROUPS:
            for c0 in range(0, width, 512):
                cw = min(512, width - c0)
                w_ref[:, dst + c0:dst + c0 + cw] = wraw_ref[src + c0:src + c0 + cw, :].T.astype(BF16)
        pad0 = W_DT + 2 * SSD_HEADS
        w_ref[:, pad0:PW] = jnp.zeros((D_MODEL, PW - pad0), BF16)

    has_prev = jnp.logical_and(i != 0, i != n_ctx_tiles).astype(F32)
    has_next = jnp.logical_and(i != n_ctx_tiles - 1, i != n_tiles - 1).astype(F32)

    def norm_mod(x):
        return _rmsnorm(x, nw_ref[...]) * (1.0 + mod_ref[1:2, :]) + mod_ref[0:1, :]

    x_tile = x_ref[...]
    if split:
        x_tile = jnp.where(i < n_ctx_tiles, ctx_ref[...], x_tile)
    h32 = norm_mod(x_tile)
    h = h32.astype(BF16)
    h_ext = jnp.concatenate([h32, norm_mod(xp_ref[...]), norm_mod(xn_ref[...])], axis=0).astype(BF16)
    prev_row, next_row = ROW_TILE + HALO - 1, ROW_TILE + HALO

    def proj(col, width):
        return _dot(h, w_ref[:, col:col + width])

    def proj_ext(col, width):
        pe = _dot(h_ext, w_ref[:, col:col + width])
        return pe[0:ROW_TILE], pe[prev_row:prev_row + 1], pe[next_row:next_row + 1]

    half = SSD_WIDTH // 2
    for k in range(2):
        p, pv, nx = proj_ext(W_X + k * half, half)
        y = _conv3(p, pv * has_prev, nx * has_next, cwx_ref.at[:, k * half:(k + 1) * half])
        f_ref[:, F_XC + k * half:F_XC + (k + 1) * half] = _silu(y + cbx_ref[:, k * half:(k + 1) * half])
    p, pv, nx = proj_ext(W_BC, BC_WIDTH)
    y = _conv3(p, pv * has_prev, nx * has_next, cwbc_ref) + cbbc_ref[...]
    b_ref[:, B_BC:B_BC + BC_WIDTH] = _silu(y).astype(BF16)

    w = SC_WIDTH
    vc, vcp, vcn = proj_ext(W_VC, 2 * w)
    conv = _conv3(vc[:, 0:w] * vc[:, w:2 * w], vcp[:, 0:w] * vcp[:, w:2 * w] * has_prev,
                  vcn[:, 0:w] * vcn[:, w:2 * w] * has_next, cwsc_ref)
    bz = proj(W_BZ, 2 * w)
    b_ref[:, B_YC:B_YC + w] = (bz[:, 0:w] * conv * _silu(bz[:, w:2 * w])).astype(BF16)

    for k in range(2):
        f_ref[:, F_ZS + k * half:F_ZS + (k + 1) * half] = _silu(proj(W_ZS + k * half, half))
    f_ref[:, F_GA:F_GA + ATTN_WIDTH] = _silu(proj(W_ZA, ATTN_WIDTH))
    f_ref[:, F_DT:F_DT + LANES] = proj(W_DT, LANES)

    cos, sin = rope_ref[:, 0:LANES], rope_ref[:, LANES:2 * LANES]
    left = lax.broadcasted_iota(jnp.int32, (ROW_TILE, LANES), 1) < HEAD_DIM
    q = proj(W_Q, ATTN_WIDTH)
    scale = HEAD_DIM ** -0.5
    for j in range(ATTN_WIDTH // LANES):
        qs = _rope(q[:, j * LANES:(j + 1) * LANES], cos, sin) * scale
        qr = pltpu.roll(qs, HEAD_DIM, 1)
        if j // 2 == 0:
            h0, h1 = jnp.where(left, qs, 0.0), jnp.where(left, qr, 0.0)
        else:
            h0, h1 = jnp.where(left, 0.0, qr), jnp.where(left, 0.0, qs)
        b_ref[:, B_Q + 2 * j * LANES:B_Q + (2 * j + 1) * LANES] = h0.astype(BF16)
        b_ref[:, B_Q + (2 * j + 1) * LANES:B_Q + (2 * j + 2) * LANES] = h1.astype(BF16)
    kv = proj(W_KV, 2 * LANES)
    b_ref[:, B_KV:B_KV + LANES] = _rope(kv[:, 0:LANES], cos, sin).astype(BF16)
    b_ref[:, B_KV + LANES:B_KV + 2 * LANES] = kv[:, LANES:2 * LANES].astype(BF16)


def _inproj(xa, ctx, modsel, norm_w, w_in, layer, consts, rope_tab, n_ctx_tiles):
    split = ctx is not None
    b, rows, d = xa.shape
    off = n_ctx_tiles if split else 0
    t = rows + off * ROW_TILE
    n_tiles = t // ROW_TILE
    per = ROW_TILE // HALO
    n_halo = rows // HALO

    def const(shape):
        return pl.BlockSpec(shape, lambda bi, i: (0,) * len(shape))

    in_specs = [
            pl.BlockSpec((None, ROW_TILE, d), lambda bi, i: (bi, jnp.maximum(i - off, 0), 0)),
            pl.BlockSpec((None, HALO, d),
                         lambda bi, i: (bi, jnp.maximum((i - off) * per - 1, 0), 0)),
            pl.BlockSpec((None, HALO, d),
                         lambda bi, i: (bi, jnp.clip((i - off + 1) * per, 0, n_halo - 1), 0)),
            pl.BlockSpec((None, None, 3, d),
                         lambda bi, i: (bi, jnp.where(i >= n_ctx_tiles, 1, 0), 0, 0)),
            const((1, d)),
            pl.BlockSpec((None, IN_COLS, d), lambda bi, i: (layer, 0, 0), pipeline_mode=pl.Buffered(1)),
            const((3, SSD_WIDTH)), const((1, SSD_WIDTH)), const((3, BC_WIDTH)), const((1, BC_WIDTH)),
            const((3, SC_WIDTH)),
            pl.BlockSpec((ROW_TILE, 2 * LANES), lambda bi, i: (i, 0)),
    ]
    args = [xa, xa, xa, modsel, norm_w.reshape(1, d), w_in, consts["cw_x"], consts["cb_x"],
            consts["cw_bc"], consts["cb_bc"], consts["cw_sc"], rope_tab]
    if split:
        assert ctx.shape[1] == ROW_TILE and n_ctx_tiles == 1
        in_specs.insert(0, pl.BlockSpec((None, ROW_TILE, d), lambda bi, i: (bi, 0, 0)))
        args.insert(0, ctx)
    return pl.pallas_call(
        functools.partial(_inproj_kernel, split, n_ctx_tiles, n_tiles),
        grid=(b, n_tiles),
        in_specs=in_specs,
        out_specs=[
            pl.BlockSpec((None, ROW_TILE, FW), lambda bi, i: (bi, i, 0)),
            pl.BlockSpec((None, ROW_TILE, BW), lambda bi, i: (bi, i, 0)),
        ],
        out_shape=[jax.ShapeDtypeStruct((b, t, FW), F32), jax.ShapeDtypeStruct((b, t, BW), BF16)],
        scratch_shapes=[pltpu.VMEM((d, PW), BF16)],
        compiler_params=pltpu.CompilerParams(
            dimension_semantics=("arbitrary", "arbitrary"), vmem_limit_bytes=VMEM_LIMIT),
        name="in_projection",
    )(*args)


def _scan_chunk(step, direction, n_ctx_chunks, n_chunks):
    if direction == 0:
        return step
    return jnp.where(step < n_ctx_chunks, n_ctx_chunks - 1 - step,
                     n_chunks - 1 - (step - n_ctx_chunks))


def _ssd_prepare(direction, xc, dt_raw, dtb_ref, alog_ref, e_ref):
    lane = lax.broadcasted_iota(jnp.int32, (1, LANES), 1)
    mine = jnp.logical_and(lane >= SSD_HEADS * direction, lane < SSD_HEADS * (direction + 1))
    a_row = jnp.where(mine, -jnp.exp(alog_ref[...]), 0.0)
    dt = _softplus(dt_raw + dtb_ref[...])
    da = dt * a_row

    r_i = lax.broadcasted_iota(jnp.int32, (CHUNK, CHUNK), 0)
    c_i = lax.broadcasted_iota(jnp.int32, (CHUNK, CHUNK), 1)
    tri = (c_i <= r_i) if direction == 0 else (c_i >= r_i)
    tri_b = jnp.where(tri, 1.0, 0.0).astype(BF16)
    u3 = _dot(tri_b, jnp.concatenate(_split3(da), axis=1))
    u = u3[:, 0:LANES] + u3[:, LANES:2 * LANES] + u3[:, 2 * LANES:3 * LANES]
    tot_row = CHUNK - 1 if direction == 0 else 0
    tot = u[tot_row:tot_row + 1, :]
    exp_u = jnp.exp(u)
    exp_d = jnp.exp(tot - u)

    stack = jnp.concatenate((dt.astype(BF16), exp_d.astype(BF16)) + _split2(exp_u), axis=0)
    ex = _dot(stack, e_ref[...])
    xdt = xc * ex[0:CHUNK]
    xdec = xdt * ex[CHUNK:2 * CHUNK]
    eu_e = ex[2 * CHUNK:3 * CHUNK] + ex[3 * CHUNK:4 * CHUNK]
    return dict(direction=direction, tri=tri, u=u, u_t=u.T, xdt=xdt, xdec=xdec, eu_e=eu_e,
                tot_row=tot_row)


def _ssd_output(prep, bc, state_ref):
    direction, tri, u, u_t = prep["direction"], prep["tri"], prep["u"], prep["u_t"]
    xdt, eu_e = prep["xdt"], prep["eu_e"]
    half = lax.broadcasted_iota(jnp.int32, (CHUNK, LANES), 1) < HEAD_DIM
    y_slabs = []
    for g in range(SSD_GROUPS):
        b_g = bc[:, g * SSD_STATE:(g + 1) * SSD_STATE]
        c_g = bc[:, (SSD_GROUPS + g) * SSD_STATE:(SSD_GROUPS + g + 1) * SSD_STATE]
        cb = _dot_nt(c_g, b_g)
        y_off = _dot(c_g, state_ref[g].astype(BF16)) * eu_e[:, g * GROUP_W:(g + 1) * GROUP_W]
        for j in range(GROUP_W // LANES):
            ms = []
            for hh in range(2):
                hl = SSD_HEADS * direction + g * (SSD_HEADS // SSD_GROUPS) + 2 * j + hh
                diff = u[:, hl:hl + 1] - u_t[hl:hl + 1, :]
                ms.append(cb * jnp.exp(jnp.where(tri, diff, -jnp.inf)))
            m2 = jnp.concatenate(ms, axis=1).astype(BF16)
            lo = g * GROUP_W + j * LANES
            slab = xdt[:, lo:lo + LANES]
            xbd = jnp.concatenate([jnp.where(half, slab, 0.0), jnp.where(half, 0.0, slab)],
                                  axis=0).astype(BF16)
            y_slabs.append(_dot(m2, xbd) + y_off[:, j * LANES:(j + 1) * LANES])
    return jnp.concatenate(y_slabs, axis=1)


def _ssd_update_state(prep, bc, state_ref):
    eu_e, xdec, tot_row = prep["eu_e"], prep["xdec"], prep["tot_row"]
    for g in range(SSD_GROUPS):
        b_g = bc[:, g * SSD_STATE:(g + 1) * SSD_STATE]
        decay = eu_e[tot_row:tot_row + 1, g * GROUP_W:(g + 1) * GROUP_W]
        state_ref[g] = state_ref[g] * decay + _dot(
            b_g.astype(F32).T.astype(BF16), xdec[:, g * GROUP_W:(g + 1) * GROUP_W].astype(BF16))


def _ssd_fwd_kernel(xc_ref, bc_ref, dt_ref, dtb_ref, alog_ref, e_ref, o_ref, state_ref):
    @pl.when(pl.program_id(1) == 0)
    def _():
        state_ref[...] = jnp.zeros_like(state_ref)

    for k in range(SCAN_ROWS // CHUNK):
        rows = slice(k * CHUNK, (k + 1) * CHUNK)
        prep = _ssd_prepare(0, xc_ref[rows, :], dt_ref[rows, :], dtb_ref, alog_ref, e_ref)
        bc = bc_ref[rows, :]
        o_ref[rows, :] = _ssd_output(prep, bc, state_ref)
        _ssd_update_state(prep, bc, state_ref)


def _ssd_forward(pf, pb, consts):
    b, t, _ = pf.shape
    return pl.pallas_call(
        _ssd_fwd_kernel,
        grid=(b, t // SCAN_ROWS),
        in_specs=[
            pl.BlockSpec((None, SCAN_ROWS, SSD_WIDTH), lambda bi, s: (bi, s, F_XC // SSD_WIDTH)),
            pl.BlockSpec((None, SCAN_ROWS, BC_WIDTH), lambda bi, s: (bi, s, B_BC // BC_WIDTH)),
            pl.BlockSpec((None, SCAN_ROWS, LANES), lambda bi, s: (bi, s, F_DT // LANES)),
            pl.BlockSpec((None, 1, LANES), lambda bi, s: (0, 0, 0)),
            pl.BlockSpec((None, 1, LANES), lambda bi, s: (0, 0, 0)),
            pl.BlockSpec((None, LANES, SSD_WIDTH), lambda bi, s: (0, 0, 0)),
        ],
        out_specs=pl.BlockSpec((None, SCAN_ROWS, SSD_WIDTH), lambda bi, s: (bi, s, 0)),
        out_shape=jax.ShapeDtypeStruct((b, t, SSD_WIDTH), F32),
        scratch_shapes=[pltpu.VMEM((SSD_GROUPS, SSD_STATE, GROUP_W), F32)],
        compiler_params=pltpu.CompilerParams(
            dimension_semantics=("parallel", "arbitrary"), vmem_limit_bytes=VMEM_LIMIT),
        name="ssd_forward",
    )(pf, pb, pf, consts["dt_bias"], consts["a_log"], consts["expand"])


def _attn_scores(q_ref, rows, k_all, bias):
    scores = []
    for j in range(ATTN_HEADS // 2):
        q2 = jnp.concatenate([q_ref[rows, 2 * j * LANES:(2 * j + 1) * LANES],
                              q_ref[rows, (2 * j + 1) * LANES:(2 * j + 2) * LANES]], axis=0)
        scores.append(_dot_nt(q2, k_all) + bias)
    return scores


def _attn_probs(scores, sink_ref):
    n_keys = scores[0].shape[1]
    probs, sink_terms = [], []
    for j, s in enumerate(scores):
        sk = jnp.concatenate([
            jnp.broadcast_to(sink_ref[2 * j:2 * j + 1, :], (CHUNK, LANES)),
            jnp.broadcast_to(sink_ref[2 * j + 1:2 * j + 2, :], (CHUNK, LANES))], axis=0)
        m = jnp.maximum(jnp.broadcast_to(jnp.max(s, axis=-1, keepdims=True), (2 * CHUNK, LANES)), sk)
        probs.append(jnp.concatenate(
            [jnp.exp(s[:, i * LANES:(i + 1) * LANES] - m).astype(BF16) for i in range(n_keys // LANES)],
            axis=1))
        sink_terms.append(jnp.exp(sk - m))
    return probs, sink_terms


def _attn_values(probs, sink_terms, v_all):
    left = lax.broadcasted_iota(jnp.int32, (CHUNK, LANES), 1) < HEAD_DIM
    v_ones = jnp.concatenate([v_all, jnp.ones((v_all.shape[0], LANES), BF16)], axis=1)
    outs = []
    for j in range(ATTN_HEADS // 2):
        pv = _dot(probs[j], v_ones)
        o = pv[:, 0:LANES] / (pv[:, LANES:2 * LANES] + sink_terms[j])
        o0, o1 = o[0:CHUNK], o[CHUNK:2 * CHUNK]
        if j // 2 == 0:
            outs.append(jnp.where(left, o0, pltpu.roll(o1, HEAD_DIM, 1)))
        else:
            outs.append(jnp.where(left, pltpu.roll(o0, HEAD_DIM, 1), o1))
    return outs


def _mix_bwd_kernel(final, split, n_ctx_blocks, n_blocks, *refs):
    if split:
        ctx_ref, refs = refs[0], refs[1:]
    (xc_ref, bc_ref, dt_ref, zs_ref, yf_ref, yc_ref, ga_ref, q_ref, kv_ref, kvp_ref, kvn_ref,
     kvc_ref, x_ref, mod_ref, wraw_ref, dtb_ref, alog_ref, e_ref, dsk_ref, nw_ref, sink_ref,
     bias_ref) = refs[:22]
    if final:
        fw_ref, o_ref, state_ref, ycat_ref, w_ref = refs[22:]
    else:
        o_ref, state_ref, ycat_ref, w_ref = refs[22:]

    step = pl.program_id(1)

    @pl.when(jnp.logical_and(pl.program_id(0) == 0, step == 0))
    def _():
        for r0 in range(0, MIX_WIDTH, 512):
            w_ref[r0:r0 + 512, :] = wraw_ref[r0:r0 + 512, :].astype(BF16)

    blk = _scan_chunk(step, 1, n_ctx_blocks, n_blocks)
    per = SCAN_ROWS // CHUNK
    n_ctx_chunks, n_chunks = per * n_ctx_blocks, per * n_blocks

    @pl.when(step == 0)
    def _():
        state_ref[...] = jnp.zeros_like(state_ref)

    col = lax.broadcasted_iota(jnp.int32, (1, bias_ref.shape[1]), 1)
    halves = [kvp_ref] + [kv_ref.at[i * CHUNK:(i + 1) * CHUNK, :] for i in range(per)] + [kvn_ref]
    for k in reversed(range(per)):
        rows = slice(k * CHUNK, (k + 1) * CHUNK)
        c = blk * per + k
        xc = xc_ref[rows, :]
        bc = bc_ref[rows, :]
        prep = _ssd_prepare(1, xc, dt_ref[rows, :], dtb_ref, alog_ref, e_ref)

        y = _ssd_output(prep, bc, state_ref) + yf_ref[rows, :] + xc * dsk_ref[...]
        gated = y * zs_ref[rows, :]
        for g in range(SSD_GROUPS):
            gg = gated[:, g * GROUP_W:(g + 1) * GROUP_W]
            ms = jnp.mean(gg * gg, axis=-1, keepdims=True)
            ycat_ref[rows, g * GROUP_W:(g + 1) * GROUP_W] = (
                gg * lax.rsqrt(ms + EPS) * nw_ref[:, g * GROUP_W:(g + 1) * GROUP_W]).astype(BF16)
        _ssd_update_state(prep, bc, state_ref)

        win = halves[k:k + 3] + [kvc_ref]
        k_all = jnp.concatenate([r[:, 0:LANES] for r in win], axis=0)
        v_all = jnp.concatenate([r[:, LANES:2 * LANES] for r in win], axis=0)
        latent = c >= n_ctx_chunks
        no_prev = jnp.where(c > n_ctx_chunks, 0.0, NEG_INF)
        no_cur = jnp.where(latent, 0.0, NEG_INF)
        no_next = jnp.where(jnp.logical_and(latent, c < n_chunks - 1), 0.0, NEG_INF)
        edge = jnp.where(col < CHUNK, no_prev,
                         jnp.where(col < 2 * CHUNK, no_cur, jnp.where(col < 3 * CHUNK, no_next, 0.0)))
        scores = _attn_scores(q_ref, rows, k_all, bias_ref[...] + edge)
        probs, sink_terms = _attn_probs(scores, sink_ref)
        slabs = _attn_values(probs, sink_terms, v_all)
        ycat_ref[rows, SSD_WIDTH + SC_WIDTH:MIX_WIDTH] = (
            jnp.concatenate(slabs, axis=1) * ga_ref[rows, :]).astype(BF16)

    acc = _dot(ycat_ref[:, 0:SSD_WIDTH], w_ref[0:SSD_WIDTH, :])
    acc = acc + _dot(yc_ref[...], w_ref[SSD_WIDTH:SSD_WIDTH + SC_WIDTH, :])
    acc = acc + _dot(ycat_ref[:, SSD_WIDTH + SC_WIDTH:MIX_WIDTH], w_ref[SSD_WIDTH + SC_WIDTH:MIX_WIDTH, :])
    x_in = x_ref[...]
    if split:
        x_in = jnp.where(blk < n_ctx_blocks, ctx_ref[...], x_in)
    xn = x_in + mod_ref[2:3, :] * acc
    if final:
        xn = _rmsnorm(xn, fw_ref[...])
    o_ref[...] = xn


def _mix_backward(pf, pb, yf, xa, ctx, modsel, w_out, layer, consts, n_ctx_blocks, final_norm_w=None):
    split = ctx is not None
    b, t, _ = pf.shape
    d = xa.shape[-1]
    n_blocks = t // SCAN_ROWS
    per = SCAN_ROWS // CHUNK
    n_chunks = per * n_blocks
    final = final_norm_w is not None
    n_ctx = n_ctx_blocks * SCAN_ROWS
    x_off = n_ctx_blocks if split else 0

    def bidx(s):
        return _scan_chunk(s, 1, n_ctx_blocks, n_blocks)

    def blk(width, offset):
        return pl.BlockSpec((None, SCAN_ROWS, width), lambda bi, s: (bi, bidx(s), offset // width))

    def const(shape):
        return pl.BlockSpec(shape, lambda bi, s: (0,) * len(shape))

    kvw = 2 * LANES
    in_specs = [
        blk(SSD_WIDTH, F_XC), blk(BC_WIDTH, B_BC), blk(LANES, F_DT), blk(SSD_WIDTH, F_ZS),
        blk(SSD_WIDTH, 0),
        blk(SC_WIDTH, B_YC), blk(ATTN_WIDTH, F_GA), blk(2 * ATTN_WIDTH, B_Q),
        blk(kvw, B_KV),
        pl.BlockSpec((None, CHUNK, kvw),
                     lambda bi, s: (bi, jnp.maximum(bidx(s) * per - 1, 0), B_KV // kvw)),
        pl.BlockSpec((None, CHUNK, kvw),
                     lambda bi, s: (bi, jnp.minimum((bidx(s) + 1) * per, n_chunks - 1), B_KV // kvw)),
        pl.BlockSpec((None, n_ctx, kvw), lambda bi, s: (bi, 0, B_KV // kvw)),
        pl.BlockSpec((None, SCAN_ROWS, d),
                     lambda bi, s: (bi, jnp.maximum(bidx(s) - x_off, 0), 0)),
        pl.BlockSpec((None, None, 3, d),
                     lambda bi, s: (bi, jnp.where(bidx(s) >= n_ctx_blocks, 1, 0), 0, 0)),
        pl.BlockSpec((None, MIX_WIDTH, d), lambda bi, s: (layer, 0, 0), pipeline_mode=pl.Buffered(1)),
        pl.BlockSpec((None, 1, LANES), lambda bi, s: (1, 0, 0)),
        pl.BlockSpec((None, 1, LANES), lambda bi, s: (1, 0, 0)),
        pl.BlockSpec((None, LANES, SSD_WIDTH), lambda bi, s: (1, 0, 0)),
        const((1, SSD_WIDTH)), const((1, SSD_WIDTH)), const((ATTN_HEADS, LANES)),
        const((2 * CHUNK, 3 * CHUNK + n_ctx)),
    ]
    args = [pf, pb, pf, pf, yf, pb, pf, pb, pb, pb, pb, pb, xa, modsel, w_out,
            consts["dt_bias"], consts["a_log"], consts["expand"], consts["d_skip"],
            consts["ssd_norm_w"], consts["sink"], consts["band_bias"]]
    if split:
        assert ctx.shape[1] == SCAN_ROWS and n_ctx_blocks == 1
        in_specs.insert(0, pl.BlockSpec((None, SCAN_ROWS, d), lambda bi, s: (bi, 0, 0)))
        args.insert(0, ctx)
    if final:
        in_specs.append(const((1, d)))
        args.append(final_norm_w.reshape(1, d))
        out_spec = pl.BlockSpec(
            (None, SCAN_ROWS, d),
            lambda bi, s: (bi, jnp.where(bidx(s) >= n_ctx_blocks, bidx(s), n_blocks - 1) - n_ctx_blocks, 0))
        out_shape = jax.ShapeDtypeStruct((b, t - n_ctx, d), F32)
    else:
        out_spec = blk(d, 0)
        out_shape = jax.ShapeDtypeStruct((b, t, d), F32)
    return pl.pallas_call(
        functools.partial(_mix_bwd_kernel, final, split, n_ctx_blocks, n_blocks),
        grid=(b, n_blocks),
        in_specs=in_specs,
        out_specs=out_spec,
        out_shape=out_shape,
        scratch_shapes=[pltpu.VMEM((SSD_GROUPS, SSD_STATE, GROUP_W), F32),
                        pltpu.VMEM((SCAN_ROWS, MIX_WIDTH), BF16),
                        pltpu.VMEM((MIX_WIDTH, d), BF16)],
        compiler_params=pltpu.CompilerParams(
            dimension_semantics=("arbitrary", "arbitrary"), vmem_limit_bytes=VMEM_LIMIT),
        name="mix_backward_final" if final else "mix_backward",
    )(*args)


def _rope_table(n_ctx, n_lat):
    n_rows = n_lat // GRID_W
    rows = np.repeat(np.arange(n_rows), GRID_W).astype(np.float32)
    cols = np.tile(np.arange(GRID_W), n_rows).astype(np.float32)
    axis_dim = HEAD_DIM // 2
    inv_freq = np.float32(ROPE_BASE) ** (-np.arange(0, axis_dim, 2, dtype=np.float32) / np.float32(axis_dim))
    ang = np.concatenate([rows[:, None] * inv_freq, cols[:, None] * inv_freq], axis=-1)
    cos, sin = np.cos(ang).astype(np.float32), np.sin(ang).astype(np.float32)
    q = HEAD_DIM // 4
    cos_h = np.concatenate([cos[:, 0:q], cos[:, 0:q], cos[:, q:2 * q], cos[:, q:2 * q]], axis=1)
    sin_h = np.concatenate([-sin[:, 0:q], sin[:, 0:q], -sin[:, q:2 * q], sin[:, q:2 * q]], axis=1)
    tab = np.concatenate([cos_h, cos_h, sin_h, sin_h], axis=1)
    ident = np.concatenate([np.ones((n_ctx, LANES), np.float32), np.zeros((n_ctx, LANES), np.float32)], axis=1)
    return jnp.asarray(np.concatenate([ident, tab], axis=0))


def _expand_matrix():
    e = np.zeros((2, LANES, SSD_WIDTH), np.float32)
    for d in range(2):
        for h in range(SSD_HEADS):
            e[d, d * SSD_HEADS + h, h * HEAD_DIM:(h + 1) * HEAD_DIM] = 1.0
    return jnp.asarray(e, BF16)


def _band_bias(n_ctx):
    row = np.arange(2 * CHUNK)[:, None] % CHUNK
    col = np.arange(3 * CHUNK + n_ctx)[None, :]
    ok = (col >= 3 * CHUNK) | ((col >= row) & (col <= row + 2 * CHUNK))
    return jnp.asarray(np.where(ok, 0.0, NEG_INF), F32)


def _dir_rows(v):
    out = jnp.zeros((2, 1, LANES), F32)
    for d in range(2):
        out = out.at[d, 0, d * SSD_HEADS:(d + 1) * SSD_HEADS].set(v[d])
    return out


def kernel(x, c, ctx, c_ctx, norm_w, w_mod, b_mod, w_in, ssd_conv_w, ssd_conv_b, ssd_dt_bias,
           ssd_a_log, ssd_d, ssd_norm_w, sc_conv_w, attn_sink, w_out, final_norm_w):
    b, n_lat, d = x.shape
    n_ctx = ctx.shape[1]
    depth = w_in.shape[0]
    assert d == D_MODEL and n_ctx % ROW_TILE == 0 and n_lat % ROW_TILE == 0 and b <= 7
    n_ctx_chunks = n_ctx // CHUNK
    n_ctx_tiles = n_ctx // ROW_TILE

    cvec = jnp.zeros((8, d), F32).at[0:b].set(c).at[b].set(c_ctx)
    mod = _modulation(cvec, w_mod, b_mod)
    mod = mod.reshape(depth, 8, 3, d)
    modsel = jnp.stack([jnp.broadcast_to(mod[:, b][:, None], (depth, b, 3, d)), mod[:, 0:b]], axis=2)

    xa, ctx_in = x, ctx
    assert n_ctx == ROW_TILE == SCAN_ROWS
    n_ctx_blocks = n_ctx // SCAN_ROWS
    rope_tab = _rope_table(n_ctx, n_lat)
    expand = _expand_matrix()
    band_bias = _band_bias(n_ctx)

    w_in_t = jnp.swapaxes(w_in, 1, 2)

    out = None
    for l in range(depth):
        last = l == depth - 1
        cw = ssd_conv_w[l]
        cbias = ssd_conv_b[l]
        consts = {
            "cw_x": cw[:, 0:SSD_WIDTH], "cb_x": cbias[0:SSD_WIDTH].reshape(1, -1),
            "cw_bc": cw[:, SSD_WIDTH:], "cb_bc": cbias[SSD_WIDTH:].reshape(1, -1),
            "cw_sc": sc_conv_w[l],
            "dt_bias": _dir_rows(ssd_dt_bias[l]), "a_log": _dir_rows(ssd_a_log[l]),
            "expand": expand,
            "d_skip": jnp.repeat(ssd_d[l], HEAD_DIM).reshape(1, -1),
            "ssd_norm_w": ssd_norm_w[l].reshape(1, -1),
            "sink": jnp.broadcast_to(attn_sink[l][:, None], (ATTN_HEADS, LANES)),
            "band_bias": band_bias,
        }
        pf, pb = _inproj(xa, ctx_in, modsel[l], norm_w[l], w_in_t, l, consts, rope_tab, n_ctx_tiles)
        yf = _ssd_forward(pf, pb, consts)
        if last:
            out = _mix_backward(pf, pb, yf, xa, ctx_in, modsel[l], w_out, l, consts, n_ctx_blocks,
                                final_norm_w)
        else:
            xa = _mix_backward(pf, pb, yf, xa, ctx_in, modsel[l], w_out, l, consts, n_ctx_blocks)
            ctx_in = None
    return out
```

```python
import functools

import numpy as np
import jax
import jax.numpy as jnp
from jax import lax
from jax.experimental import pallas as pl
from jax.experimental.pallas import tpu as pltpu

F32 = jnp.float32
BF16 = jnp.bfloat16

EPS = 1e-6
NEG_INF = -1e30
ROPE_BASE = 10000.0
GRID_W = 64

D_MODEL = 1024
CHUNK = 128
HALO = 8
ROW_TILE = 256
SSD_WIDTH = 1024
SSD_HEADS = 16
SSD_GROUPS = 2
SSD_STATE = 128
GROUP_W = SSD_WIDTH // SSD_GROUPS
BC_WIDTH = 2 * SSD_GROUPS * SSD_STATE
SC_WIDTH = 512
ATTN_WIDTH = 512
ATTN_HEADS = 8
ATTN_REP = 4
HEAD_DIM = 64
MIX_WIDTH = 2048
LANES = 128

W_X = 0
W_BC = 1024
W_VC = 1536
W_ZS = 2560
W_BZ = 3584
W_Q = 4608
W_ZA = 5120
W_KV = 5632
W_DT = 5888
PW = 6016
_W_GROUPS = ((1024, W_X, 1024), (2048, W_BC, 512), (2592, W_VC, 1024), (0, W_ZS, 1024),
             (3616, W_BZ, 1024), (4640, W_Q, 512), (5408, W_ZA, 512), (5152, W_KV, 256),
             (2560, W_DT, 32))
IN_COLS = 5920

F_XC = 0
F_ZS = 1024
F_GA = 2048
F_DT = 2560
FW = 2688
B_Q = 0
B_BC = 1024
B_YC = 1536
B_KV = 2048
BW = 2304
SCAN_ROWS = 2 * CHUNK

VMEM_LIMIT = 56 * 1024 * 1024


def _sigmoid(v):
    return 1.0 / (1.0 + jnp.exp(-v))


def _silu(v):
    return v * _sigmoid(v)


def _softplus(v):
    return jnp.maximum(v, 0.0) + jnp.log1p(jnp.exp(-jnp.abs(v)))


def _split2(v):
    hi = v.astype(BF16)
    lo = (v - hi.astype(F32)).astype(BF16)
    return hi, lo


def _split3(v):
    hi = v.astype(BF16)
    r = v - hi.astype(F32)
    mid = r.astype(BF16)
    lo = (r - mid.astype(F32)).astype(BF16)
    return hi, mid, lo


def _dot(a, b):
    return jnp.dot(a, b, preferred_element_type=F32)


def _dot_nt(a, b):
    return lax.dot_general(a, b, (((1,), (1,)), ((), ())), preferred_element_type=F32)


def _rmsnorm(x, w):
    ms = jnp.mean(x * x, axis=-1, keepdims=True)
    return x * lax.rsqrt(ms + EPS) * w


def _mod_kernel(c_ref, w_ref, b_ref, o_ref):
    sc = _silu(c_ref[...])
    o_ref[...] = _dot(sc.astype(BF16), w_ref[...].astype(BF16)) + b_ref[...]


def _modulation(cvec, w_mod, b_mod):
    depth = w_mod.shape[0]
    nblk = w_mod.shape[2] // D_MODEL
    return pl.pallas_call(
        _mod_kernel,
        grid=(depth, nblk),
        in_specs=[
            pl.BlockSpec((8, D_MODEL), lambda l, j: (0, 0)),
            pl.BlockSpec((None, D_MODEL, D_MODEL), lambda l, j: (l, 0, j)),
            pl.BlockSpec((None, 1, D_MODEL), lambda l, j: (l, 0, j)),
        ],
        out_specs=pl.BlockSpec((None, 8, D_MODEL), lambda l, j: (l, 0, j)),
        out_shape=jax.ShapeDtypeStruct((depth, 8, w_mod.shape[2]), F32),
        compiler_params=pltpu.CompilerParams(vmem_limit_bytes=VMEM_LIMIT),
        name="modulation",
    )(cvec, w_mod, b_mod.reshape(depth, 1, -1))


def _conv3(t, prev_row, next_row, w_ref):
    n = t.shape[0]
    rows = lax.broadcasted_iota(jnp.int32, t.shape, 0)
    tp = jnp.where(rows == 0, prev_row, pltpu.roll(t, 1, 0))
    tn = jnp.where(rows == n - 1, next_row, pltpu.roll(t, n - 1, 0))
    return w_ref[0:1, :] * tp + w_ref[1:2, :] * t + w_ref[2:3, :] * tn


def _rope(t, cos, sin_signed):
    lane = lax.broadcasted_iota(jnp.int32, t.shape, 1)
    first = (lane % 32) < 16
    swapped = jnp.where(first, pltpu.roll(t, LANES - 16, 1), pltpu.roll(t, 16, 1))
    return t * cos + swapped * sin_signed


def _inproj_kernel(split, n_ctx_tiles, n_tiles, *refs):
    if split:
        ctx_ref, refs = refs[0], refs[1:]
    (x_ref, xp_ref, xn_ref, mod_ref, nw_ref, wraw_ref, cwx_ref, cbx_ref, cwbc_ref,
     cbbc_ref, cwsc_ref, rope_ref, f_ref, b_ref, w_ref) = refs
    i = pl.program_id(1)

    @pl.when(jnp.logical_and(pl.program_id(0) == 0, i == 0))
    def _():
        for src, dst, width in _W_GROUPS:
            for c0 in range(0, width, 512):
                cw = min(512, width - c0)
                w_ref[:, dst + c0:dst + c0 + cw] = wraw_ref[src + c0:src + c0 + cw, :].T.astype(BF16)
        pad0 = W_DT + 2 * SSD_HEADS
        w_ref[:, pad0:PW] = jnp.zeros((D_MODEL, PW - pad0), BF16)

    has_prev = jnp.logical_and(i != 0, i != n_ctx_tiles).astype(F32)
    has_next = jnp.logical_and(i != n_ctx_tiles - 1, i != n_tiles - 1).astype(F32)

    def norm_mod(x):
        return _rmsnorm(x, nw_ref[...]) * (1.0 + mod_ref[1:2, :]) + mod_ref[0:1, :]

    x_tile = x_ref[...]
    if split:
        x_tile = jnp.where(i < n_ctx_tiles, ctx_ref[...], x_tile)
    h32 = norm_mod(x_tile)
    h = h32.astype(BF16)
    h_ext = jnp.concatenate([h32, norm_mod(xp_ref[...]), norm_mod(xn_ref[...])], axis=0).astype(BF16)
    prev_row, next_row = ROW_TILE + HALO - 1, ROW_TILE + HALO

    def proj(col, width):
        return _dot(h, w_ref[:, col:col + width])

    def proj_ext(col, width):
        pe = _dot(h_ext, w_ref[:, col:col + width])
        return pe[0:ROW_TILE], pe[prev_row:prev_row + 1], pe[next_row:next_row + 1]

    half = SSD_WIDTH // 2
    for k in range(2):
        p, pv, nx = proj_ext(W_X + k * half, half)
        y = _conv3(p, pv * has_prev, nx * has_next, cwx_ref.at[:, k * half:(k + 1) * half])
        f_ref[:, F_XC + k * half:F_XC + (k + 1) * half] = _silu(y + cbx_ref[:, k * half:(k + 1) * half])
    p, pv, nx = proj_ext(W_BC, BC_WIDTH)
    y = _conv3(p, pv * has_prev, nx * has_next, cwbc_ref) + cbbc_ref[...]
    b_ref[:, B_BC:B_BC + BC_WIDTH] = _silu(y).astype(BF16)

    w = SC_WIDTH
    vc, vcp, vcn = proj_ext(W_VC, 2 * w)
    conv = _conv3(vc[:, 0:w] * vc[:, w:2 * w], vcp[:, 0:w] * vcp[:, w:2 * w] * has_prev,
                  vcn[:, 0:w] * vcn[:, w:2 * w] * has_next, cwsc_ref)
    bz = proj(W_BZ, 2 * w)
    b_ref[:, B_YC:B_YC + w] = (bz[:, 0:w] * conv * _silu(bz[:, w:2 * w])).astype(BF16)

    for k in range(2):
        f_ref[:, F_ZS + k * half:F_ZS + (k + 1) * half] = _silu(proj(W_ZS + k * half, half))
    f_ref[:, F_GA:F_GA + ATTN_WIDTH] = _silu(proj(W_ZA, ATTN_WIDTH))
    f_ref[:, F_DT:F_DT + LANES] = proj(W_DT, LANES)

    cos, sin = rope_ref[:, 0:LANES], rope_ref[:, LANES:2 * LANES]
    left = lax.broadcasted_iota(jnp.int32, (ROW_TILE, LANES), 1) < HEAD_DIM
    q = proj(W_Q, ATTN_WIDTH)
    scale = HEAD_DIM ** -0.5
    for j in range(ATTN_WIDTH // LANES):
        qs = _rope(q[:, j * LANES:(j + 1) * LANES], cos, sin) * scale
        qr = pltpu.roll(qs, HEAD_DIM, 1)
        if j // 2 == 0:
            h0, h1 = jnp.where(left, qs, 0.0), jnp.where(left, qr, 0.0)
        else:
            h0, h1 = jnp.where(left, 0.0, qr), jnp.where(left, 0.0, qs)
        b_ref[:, B_Q + 2 * j * LANES:B_Q + (2 * j + 1) * LANES] = h0.astype(BF16)
        b_ref[:, B_Q + (2 * j + 1) * LANES:B_Q + (2 * j + 2) * LANES] = h1.astype(BF16)
    kv = proj(W_KV, 2 * LANES)
    b_ref[:, B_KV:B_KV + LANES] = _rope(kv[:, 0:LANES], cos, sin).astype(BF16)
    b_ref[:, B_KV + LANES:B_KV + 2 * LANES] = kv[:, LANES:2 * LANES].astype(BF16)


def _inproj(xa, ctx, modsel, norm_w, w_in, layer, consts, rope_tab, n_ctx_tiles):
    split = ctx is not None
    b, rows, d = xa.shape
    off = n_ctx_tiles if split else 0
    t = rows + off * ROW_TILE
    n_tiles = t // ROW_TILE
    per = ROW_TILE // HALO
    n_halo = rows // HALO

    def const(shape):
        return pl.BlockSpec(shape, lambda bi, i: (0,) * len(shape))

    in_specs = [
            pl.BlockSpec((None, ROW_TILE, d), lambda bi, i: (bi, jnp.maximum(i - off, 0), 0)),
            pl.BlockSpec((None, HALO, d),
                         lambda bi, i: (bi, jnp.maximum((i - off) * per - 1, 0), 0)),
            pl.BlockSpec((None, HALO, d),
                         lambda bi, i: (bi, jnp.clip((i - off + 1) * per, 0, n_halo - 1), 0)),
            pl.BlockSpec((None, None, 3, d),
                         lambda bi, i: (bi, jnp.where(i >= n_ctx_tiles, 1, 0), 0, 0)),
            const((1, d)),
            pl.BlockSpec((None, IN_COLS, d), lambda bi, i: (layer, 0, 0), pipeline_mode=pl.Buffered(1)),
            const((3, SSD_WIDTH)), const((1, SSD_WIDTH)), const((3, BC_WIDTH)), const((1, BC_WIDTH)),
            const((3, SC_WIDTH)),
            pl.BlockSpec((ROW_TILE, 2 * LANES), lambda bi, i: (i, 0)),
    ]
    args = [xa, xa, xa, modsel, norm_w.reshape(1, d), w_in, consts["cw_x"], consts["cb_x"],
            consts["cw_bc"], consts["cb_bc"], consts["cw_sc"], rope_tab]
    if split:
        assert ctx.shape[1] == ROW_TILE and n_ctx_tiles == 1
        in_specs.insert(0, pl.BlockSpec((None, ROW_TILE, d), lambda bi, i: (bi, 0, 0)))
        args.insert(0, ctx)
    return pl.pallas_call(
        functools.partial(_inproj_kernel, split, n_ctx_tiles, n_tiles),
        grid=(b, n_tiles),
        in_specs=in_specs,
        out_specs=[
            pl.BlockSpec((None, ROW_TILE, FW), lambda bi, i: (bi, i, 0)),
            pl.BlockSpec((None, ROW_TILE, BW), lambda bi, i: (bi, i, 0)),
        ],
        out_shape=[jax.ShapeDtypeStruct((b, t, FW), F32), jax.ShapeDtypeStruct((b, t, BW), BF16)],
        scratch_shapes=[pltpu.VMEM((d, PW), BF16)],
        compiler_params=pltpu.CompilerParams(
            dimension_semantics=("arbitrary", "arbitrary"), vmem_limit_bytes=VMEM_LIMIT),
        name="in_projection",
    )(*args)


def _scan_chunk(step, direction, n_ctx_chunks, n_chunks):
    if direction == 0:
        return step
    return jnp.where(step < n_ctx_chunks, n_ctx_chunks - 1 - step,
                     n_chunks - 1 - (step - n_ctx_chunks))


def _ssd_prepare(direction, xc, dt_raw, dtb_ref, alog_ref, e_ref):
    lane = lax.broadcasted_iota(jnp.int32, (1, LANES), 1)
    mine = jnp.logical_and(lane >= SSD_HEADS * direction, lane < SSD_HEADS * (direction + 1))
    a_row = jnp.where(mine, -jnp.exp(alog_ref[...]), 0.0)
    dt = _softplus(dt_raw + dtb_ref[...])
    da = dt * a_row

    r_i = lax.broadcasted_iota(jnp.int32, (CHUNK, CHUNK), 0)
    c_i = lax.broadcasted_iota(jnp.int32, (CHUNK, CHUNK), 1)
    tri = (c_i <= r_i) if direction == 0 else (c_i >= r_i)
    tri_b = jnp.where(tri, 1.0, 0.0).astype(BF16)
    u3 = _dot(tri_b, jnp.concatenate(_split3(da), axis=1))
    u = u3[:, 0:LANES] + u3[:, LANES:2 * LANES] + u3[:, 2 * LANES:3 * LANES]
    tot_row = CHUNK - 1 if direction == 0 else 0
    tot = u[tot_row:tot_row + 1, :]
    exp_u = jnp.exp(u)
    exp_d = jnp.exp(tot - u)

    stack = jnp.concatenate((dt.astype(BF16), exp_d.astype(BF16)) + _split2(exp_u), axis=0)
    ex = _dot(stack, e_ref[...])
    xdt = xc * ex[0:CHUNK]
    xdec = xdt * ex[CHUNK:2 * CHUNK]
    eu_e = ex[2 * CHUNK:3 * CHUNK] + ex[3 * CHUNK:4 * CHUNK]
    return dict(direction=direction, tri=tri, u=u, u_t=u.T, xdt=xdt, xdec=xdec, eu_e=eu_e,
                tot_row=tot_row)


def _ssd_output(prep, bc, state_ref):
    direction, tri, u, u_t = prep["direction"], prep["tri"], prep["u"], prep["u_t"]
    xdt, eu_e = prep["xdt"], prep["eu_e"]
    half = lax.broadcasted_iota(jnp.int32, (CHUNK, LANES), 1) < HEAD_DIM
    y_slabs = []
    for g in range(SSD_GROUPS):
        b_g = bc[:, g * SSD_STATE:(g + 1) * SSD_STATE]
        c_g = bc[:, (SSD_GROUPS + g) * SSD_STATE:(SSD_GROUPS + g + 1) * SSD_STATE]
        cb = _dot_nt(c_g, b_g)
        y_off = _dot(c_g, state_ref[g].astype(BF16)) * eu_e[:, g * GROUP_W:(g + 1) * GROUP_W]
        for j in range(GROUP_W // LANES):
            ms = []
            for hh in range(2):
                hl = SSD_HEADS * direction + g * (SSD_HEADS // SSD_GROUPS) + 2 * j + hh
                diff = u[:, hl:hl + 1] - u_t[hl:hl + 1, :]
                ms.append(cb * jnp.exp(jnp.where(tri, diff, -jnp.inf)))
            m2 = jnp.concatenate(ms, axis=1).astype(BF16)
            lo = g * GROUP_W + j * LANES
            slab = xdt[:, lo:lo + LANES]
            xbd = jnp.concatenate([jnp.where(half, slab, 0.0), jnp.where(half, 0.0, slab)],
                                  axis=0).astype(BF16)
            y_slabs.append(_dot(m2, xbd) + y_off[:, j * LANES:(j + 1) * LANES])
    return jnp.concatenate(y_slabs, axis=1)


def _ssd_update_state(prep, bc, state_ref):
    eu_e, xdec, tot_row = prep["eu_e"], prep["xdec"], prep["tot_row"]
    for g in range(SSD_GROUPS):
        b_g = bc[:, g * SSD_STATE:(g + 1) * SSD_STATE]
        decay = eu_e[tot_row:tot_row + 1, g * GROUP_W:(g + 1) * GROUP_W]
        state_ref[g] = state_ref[g] * decay + _dot(
            b_g.astype(F32).T.astype(BF16), xdec[:, g * GROUP_W:(g + 1) * GROUP_W].astype(BF16))


def _ssd_fwd_kernel(n_samples, xc_ref, bc_ref, dt_ref, dtb_ref, alog_ref, e_ref, o_ref, state_ref):
    @pl.when(pl.program_id(1) == 0)
    def _():
        state_ref[...] = jnp.zeros_like(state_ref)

    for k in range(SCAN_ROWS // CHUNK):
        rows = slice(k * CHUNK, (k + 1) * CHUNK)
        for smp in range(n_samples):
            prep = _ssd_prepare(0, xc_ref[smp, rows, :], dt_ref[smp, rows, :], dtb_ref, alog_ref, e_ref)
            bc = bc_ref[smp, rows, :]
            o_ref[smp, rows, :] = _ssd_output(prep, bc, state_ref.at[smp])
            _ssd_update_state(prep, bc, state_ref.at[smp])


def _ssd_forward(pf, pb, consts):
    b, t, _ = pf.shape
    ns = 2 if b % 2 == 0 else 1
    return pl.pallas_call(
        functools.partial(_ssd_fwd_kernel, ns),
        grid=(b // ns, t // SCAN_ROWS),
        in_specs=[
            pl.BlockSpec((ns, SCAN_ROWS, SSD_WIDTH), lambda bi, s: (bi, s, F_XC // SSD_WIDTH)),
            pl.BlockSpec((ns, SCAN_ROWS, BC_WIDTH), lambda bi, s: (bi, s, B_BC // BC_WIDTH)),
            pl.BlockSpec((ns, SCAN_ROWS, LANES), lambda bi, s: (bi, s, F_DT // LANES)),
            pl.BlockSpec((None, 1, LANES), lambda bi, s: (0, 0, 0)),
            pl.BlockSpec((None, 1, LANES), lambda bi, s: (0, 0, 0)),
            pl.BlockSpec((None, LANES, SSD_WIDTH), lambda bi, s: (0, 0, 0)),
        ],
        out_specs=pl.BlockSpec((ns, SCAN_ROWS, SSD_WIDTH), lambda bi, s: (bi, s, 0)),
        out_shape=jax.ShapeDtypeStruct((b, t, SSD_WIDTH), F32),
        scratch_shapes=[pltpu.VMEM((ns, SSD_GROUPS, SSD_STATE, GROUP_W), F32)],
        compiler_params=pltpu.CompilerParams(
            dimension_semantics=("parallel", "arbitrary"), vmem_limit_bytes=VMEM_LIMIT),
        name="ssd_forward",
    )(pf, pb, pf, consts["dt_bias"], consts["a_log"], consts["expand"])


def _attn_scores(q_ref, rows, k_all, bias):
    scores = []
    for j in range(ATTN_HEADS // 2):
        q2 = jnp.concatenate([q_ref[rows, 2 * j * LANES:(2 * j + 1) * LANES],
                              q_ref[rows, (2 * j + 1) * LANES:(2 * j + 2) * LANES]], axis=0)
        scores.append(_dot_nt(q2, k_all) + bias)
    return scores


def _attn_probs(scores, sink_ref):
    n_keys = scores[0].shape[1]
    probs, sink_terms = [], []
    for j, s in enumerate(scores):
        sk = jnp.concatenate([
            jnp.broadcast_to(sink_ref[2 * j:2 * j + 1, :], (CHUNK, LANES)),
            jnp.broadcast_to(sink_ref[2 * j + 1:2 * j + 2, :], (CHUNK, LANES))], axis=0)
        m = jnp.maximum(jnp.broadcast_to(jnp.max(s, axis=-1, keepdims=True), (2 * CHUNK, LANES)), sk)
        probs.append(jnp.concatenate(
            [jnp.exp(s[:, i * LANES:(i + 1) * LANES] - m).astype(BF16) for i in range(n_keys // LANES)],
            axis=1))
        sink_terms.append(jnp.exp(sk - m))
    return probs, sink_terms


def _attn_values(probs, sink_terms, v_all):
    left = lax.broadcasted_iota(jnp.int32, (CHUNK, LANES), 1) < HEAD_DIM
    v_ones = jnp.concatenate([v_all, jnp.ones((v_all.shape[0], LANES), BF16)], axis=1)
    outs = []
    for j in range(ATTN_HEADS // 2):
        pv = _dot(probs[j], v_ones)
        o = pv[:, 0:LANES] / (pv[:, LANES:2 * LANES] + sink_terms[j])
        o0, o1 = o[0:CHUNK], o[CHUNK:2 * CHUNK]
        if j // 2 == 0:
            outs.append(jnp.where(left, o0, pltpu.roll(o1, HEAD_DIM, 1)))
        else:
            outs.append(jnp.where(left, pltpu.roll(o0, HEAD_DIM, 1), o1))
    return outs


def _mix_bwd_kernel(final, split, n_samples, n_ctx_blocks, n_blocks, *refs):
    if split:
        ctx_ref, refs = refs[0], refs[1:]
    (xc_ref, bc_ref, dt_ref, zs_ref, yf_ref, yc_ref, ga_ref, q_ref, kv_ref, kvp_ref, kvn_ref,
     kvc_ref, x_ref, mod_ref, wraw_ref, dtb_ref, alog_ref, e_ref, dsk_ref, nw_ref, sink_ref,
     bias_ref) = refs[:22]
    if final:
        fw_ref, o_ref, state_ref, ycat_ref, w_ref = refs[22:]
    else:
        o_ref, state_ref, ycat_ref, w_ref = refs[22:]

    step = pl.program_id(1)

    @pl.when(jnp.logical_and(pl.program_id(0) == 0, step == 0))
    def _():
        for r0 in range(0, MIX_WIDTH, 512):
            w_ref[r0:r0 + 512, :] = wraw_ref[r0:r0 + 512, :].astype(BF16)

    blk = _scan_chunk(step, 1, n_ctx_blocks, n_blocks)
    per = SCAN_ROWS // CHUNK
    n_ctx_chunks, n_chunks = per * n_ctx_blocks, per * n_blocks

    @pl.when(step == 0)
    def _():
        state_ref[...] = jnp.zeros_like(state_ref)

    col = lax.broadcasted_iota(jnp.int32, (1, bias_ref.shape[1]), 1)
    for k in reversed(range(per)):
        rows = slice(k * CHUNK, (k + 1) * CHUNK)
        c = blk * per + k
        latent = c >= n_ctx_chunks
        no_prev = jnp.where(c > n_ctx_chunks, 0.0, NEG_INF)
        no_cur = jnp.where(latent, 0.0, NEG_INF)
        no_next = jnp.where(jnp.logical_and(latent, c < n_chunks - 1), 0.0, NEG_INF)
        edge = jnp.where(col < CHUNK, no_prev,
                         jnp.where(col < 2 * CHUNK, no_cur, jnp.where(col < 3 * CHUNK, no_next, 0.0)))
        bias = bias_ref[...] + edge
        for smp in range(n_samples):
            out_rows = slice(smp * SCAN_ROWS + k * CHUNK, smp * SCAN_ROWS + (k + 1) * CHUNK)
            xc = xc_ref[smp, rows, :]
            bc = bc_ref[smp, rows, :]
            prep = _ssd_prepare(1, xc, dt_ref[smp, rows, :], dtb_ref, alog_ref, e_ref)

            y = _ssd_output(prep, bc, state_ref.at[smp]) + yf_ref[smp, rows, :] + xc * dsk_ref[...]
            gated = y * zs_ref[smp, rows, :]
            for g in range(SSD_GROUPS):
                gg = gated[:, g * GROUP_W:(g + 1) * GROUP_W]
                ms = jnp.mean(gg * gg, axis=-1, keepdims=True)
                ycat_ref[out_rows, g * GROUP_W:(g + 1) * GROUP_W] = (
                    gg * lax.rsqrt(ms + EPS) * nw_ref[:, g * GROUP_W:(g + 1) * GROUP_W]).astype(BF16)
            _ssd_update_state(prep, bc, state_ref.at[smp])

            halves = ([kvp_ref.at[smp]] + [kv_ref.at[smp, i * CHUNK:(i + 1) * CHUNK, :] for i in range(per)]
                      + [kvn_ref.at[smp]])
            win = halves[k:k + 3] + [kvc_ref.at[smp]]
            k_all = jnp.concatenate([r[:, 0:LANES] for r in win], axis=0)
            v_all = jnp.concatenate([r[:, LANES:2 * LANES] for r in win], axis=0)
            scores = _attn_scores(q_ref.at[smp], rows, k_all, bias)
            probs, sink_terms = _attn_probs(scores, sink_ref)
            slabs = _attn_values(probs, sink_terms, v_all)
            ycat_ref[out_rows, SSD_WIDTH + SC_WIDTH:MIX_WIDTH] = (
                jnp.concatenate(slabs, axis=1) * ga_ref[smp, rows, :]).astype(BF16)

    yc_all = jnp.concatenate([yc_ref[smp] for smp in range(n_samples)], axis=0)
    acc = _dot(ycat_ref[:, 0:SSD_WIDTH], w_ref[0:SSD_WIDTH, :])
    acc = acc + _dot(yc_all, w_ref[SSD_WIDTH:SSD_WIDTH + SC_WIDTH, :])
    acc = acc + _dot(ycat_ref[:, SSD_WIDTH + SC_WIDTH:MIX_WIDTH], w_ref[SSD_WIDTH + SC_WIDTH:MIX_WIDTH, :])
    for smp in range(n_samples):
        x_in = x_ref[smp]
        if split:
            x_in = jnp.where(blk < n_ctx_blocks, ctx_ref[smp], x_in)
        xn = x_in + mod_ref[smp, 2:3, :] * acc[smp * SCAN_ROWS:(smp + 1) * SCAN_ROWS]
        if final:
            xn = _rmsnorm(xn, fw_ref[...])
        o_ref[smp] = xn


def _mix_backward(pf, pb, yf, xa, ctx, modsel, w_out, layer, consts, n_ctx_blocks, final_norm_w=None):
    split = ctx is not None
    b, t, _ = pf.shape
    d = xa.shape[-1]
    n_blocks = t // SCAN_ROWS
    per = SCAN_ROWS // CHUNK
    n_chunks = per * n_blocks
    final = final_norm_w is not None
    n_ctx = n_ctx_blocks * SCAN_ROWS
    x_off = n_ctx_blocks if split else 0

    def bidx(s):
        return _scan_chunk(s, 1, n_ctx_blocks, n_blocks)

    ns = 2 if b % 2 == 0 else 1

    def blk(width, offset):
        return pl.BlockSpec((ns, SCAN_ROWS, width), lambda bi, s: (bi, bidx(s), offset // width))

    def const(shape):
        return pl.BlockSpec(shape, lambda bi, s: (0,) * len(shape))

    kvw = 2 * LANES
    in_specs = [
        blk(SSD_WIDTH, F_XC), blk(BC_WIDTH, B_BC), blk(LANES, F_DT), blk(SSD_WIDTH, F_ZS),
        blk(SSD_WIDTH, 0),
        blk(SC_WIDTH, B_YC), blk(ATTN_WIDTH, F_GA), blk(2 * ATTN_WIDTH, B_Q),
        blk(kvw, B_KV),
        pl.BlockSpec((ns, CHUNK, kvw),
                     lambda bi, s: (bi, jnp.maximum(bidx(s) * per - 1, 0), B_KV // kvw)),
        pl.BlockSpec((ns, CHUNK, kvw),
                     lambda bi, s: (bi, jnp.minimum((bidx(s) + 1) * per, n_chunks - 1), B_KV // kvw)),
        pl.BlockSpec((ns, n_ctx, kvw), lambda bi, s: (bi, 0, B_KV // kvw)),
        pl.BlockSpec((ns, SCAN_ROWS, d),
                     lambda bi, s: (bi, jnp.maximum(bidx(s) - x_off, 0), 0)),
        pl.BlockSpec((ns, None, 3, d),
                     lambda bi, s: (bi, jnp.where(bidx(s) >= n_ctx_blocks, 1, 0), 0, 0)),
        pl.BlockSpec((None, MIX_WIDTH, d), lambda bi, s: (layer, 0, 0), pipeline_mode=pl.Buffered(1)),
        pl.BlockSpec((None, 1, LANES), lambda bi, s: (1, 0, 0)),
        pl.BlockSpec((None, 1, LANES), lambda bi, s: (1, 0, 0)),
        pl.BlockSpec((None, LANES, SSD_WIDTH), lambda bi, s: (1, 0, 0)),
        const((1, SSD_WIDTH)), const((1, SSD_WIDTH)), const((ATTN_HEADS, LANES)),
        const((2 * CHUNK, 3 * CHUNK + n_ctx)),
    ]
    args = [pf, pb, pf, pf, yf, pb, pf, pb, pb, pb, pb, pb, xa, modsel, w_out,
            consts["dt_bias"], consts["a_log"], consts["expand"], consts["d_skip"],
            consts["ssd_norm_w"], consts["sink"], consts["band_bias"]]
    if split:
        assert ctx.shape[1] == SCAN_ROWS and n_ctx_blocks == 1
        in_specs.insert(0, pl.BlockSpec((ns, SCAN_ROWS, d), lambda bi, s: (bi, 0, 0)))
        args.insert(0, ctx)
    if final:
        in_specs.append(const((1, d)))
        args.append(final_norm_w.reshape(1, d))
        out_spec = pl.BlockSpec(
            (ns, SCAN_ROWS, d),
            lambda bi, s: (bi, jnp.where(bidx(s) >= n_ctx_blocks, bidx(s), n_blocks - 1) - n_ctx_blocks, 0))
        out_shape = jax.ShapeDtypeStruct((b, t - n_ctx, d), F32)
    else:
        out_spec = blk(d, 0)
        out_shape = jax.ShapeDtypeStruct((b, t, d), F32)
    return pl.pallas_call(
        functools.partial(_mix_bwd_kernel, final, split, ns, n_ctx_blocks, n_blocks),
        grid=(b // ns, n_blocks),
        in_specs=in_specs,
        out_specs=out_spec,
        out_shape=out_shape,
        scratch_shapes=[pltpu.VMEM((ns, SSD_GROUPS, SSD_STATE, GROUP_W), F32),
                        pltpu.VMEM((ns * SCAN_ROWS, MIX_WIDTH), BF16),
                        pltpu.VMEM((MIX_WIDTH, d), BF16)],
        compiler_params=pltpu.CompilerParams(
            dimension_semantics=("arbitrary", "arbitrary"), vmem_limit_bytes=VMEM_LIMIT),
        name="mix_backward_final" if final else "mix_backward",
    )(*args)


def _rope_table(n_ctx, n_lat):
    n_rows = n_lat // GRID_W
    rows = np.repeat(np.arange(n_rows), GRID_W).astype(np.float32)
    cols = np.tile(np.arange(GRID_W), n_rows).astype(np.float32)
    axis_dim = HEAD_DIM // 2
    inv_freq = np.float32(ROPE_BASE) ** (-np.arange(0, axis_dim, 2, dtype=np.float32) / np.float32(axis_dim))
    ang = np.concatenate([rows[:, None] * inv_freq, cols[:, None] * inv_freq], axis=-1)
    cos, sin = np.cos(ang).astype(np.float32), np.sin(ang).astype(np.float32)
    q = HEAD_DIM // 4
    cos_h = np.concatenate([cos[:, 0:q], cos[:, 0:q], cos[:, q:2 * q], cos[:, q:2 * q]], axis=1)
    sin_h = np.concatenate([-sin[:, 0:q], sin[:, 0:q], -sin[:, q:2 * q], sin[:, q:2 * q]], axis=1)
    tab = np.concatenate([cos_h, cos_h, sin_h, sin_h], axis=1)
    ident = np.concatenate([np.ones((n_ctx, LANES), np.float32), np.zeros((n_ctx, LANES), np.float32)], axis=1)
    return jnp.asarray(np.concatenate([ident, tab], axis=0))


def _expand_matrix():
    e = np.zeros((2, LANES, SSD_WIDTH), np.float32)
    for d in range(2):
        for h in range(SSD_HEADS):
            e[d, d * SSD_HEADS + h, h * HEAD_DIM:(h + 1) * HEAD_DIM] = 1.0
    return jnp.asarray(e, BF16)


def _band_bias(n_ctx):
    row = np.arange(2 * CHUNK)[:, None] % CHUNK
    col = np.arange(3 * CHUNK + n_ctx)[None, :]
    ok = (col >= 3 * CHUNK) | ((col >= row) & (col <= row + 2 * CHUNK))
    return jnp.asarray(np.where(ok, 0.0, NEG_INF), F32)


def _dir_rows(v):
    out = jnp.zeros((2, 1, LANES), F32)
    for d in range(2):
        out = out.at[d, 0, d * SSD_HEADS:(d + 1) * SSD_HEADS].set(v[d])
    return out


def kernel(x, c, ctx, c_ctx, norm_w, w_mod, b_mod, w_in, ssd_conv_w, ssd_conv_b, ssd_dt_bias,
           ssd_a_log, ssd_d, ssd_norm_w, sc_conv_w, attn_sink, w_out, final_norm_w):
    b, n_lat, d = x.shape
    n_ctx = ctx.shape[1]
    depth = w_in.shape[0]
    assert d == D_MODEL and n_ctx % ROW_TILE == 0 and n_lat % ROW_TILE == 0 and b <= 7
    n_ctx_chunks = n_ctx // CHUNK
    n_ctx_tiles = n_ctx // ROW_TILE

    cvec = jnp.zeros((8, d), F32).at[0:b].set(c).at[b].set(c_ctx)
    mod = _modulation(cvec, w_mod, b_mod)
    mod = mod.reshape(depth, 8, 3, d)
    modsel = jnp.stack([jnp.broadcast_to(mod[:, b][:, None], (depth, b, 3, d)), mod[:, 0:b]], axis=2)

    xa, ctx_in = x, ctx
    assert n_ctx == ROW_TILE == SCAN_ROWS
    n_ctx_blocks = n_ctx // SCAN_ROWS
    rope_tab = _rope_table(n_ctx, n_lat)
    expand = _expand_matrix()
    band_bias = _band_bias(n_ctx)

    w_in_t = jnp.swapaxes(w_in, 1, 2)

    out = None
    for l in range(depth):
        last = l == depth - 1
        cw = ssd_conv_w[l]
        cbias = ssd_conv_b[l]
        consts = {
            "cw_x": cw[:, 0:SSD_WIDTH], "cb_x": cbias[0:SSD_WIDTH].reshape(1, -1),
            "cw_bc": cw[:, SSD_WIDTH:], "cb_bc": cbias[SSD_WIDTH:].reshape(1, -1),
            "cw_sc": sc_conv_w[l],
            "dt_bias": _dir_rows(ssd_dt_bias[l]), "a_log": _dir_rows(ssd_a_log[l]),
            "expand": expand,
            "d_skip": jnp.repeat(ssd_d[l], HEAD_DIM).reshape(1, -1),
            "ssd_norm_w": ssd_norm_w[l].reshape(1, -1),
            "sink": jnp.broadcast_to(attn_sink[l][:, None], (ATTN_HEADS, LANES)),
            "band_bias": band_bias,
        }
        pf, pb = _inproj(xa, ctx_in, modsel[l], norm_w[l], w_in_t, l, consts, rope_tab, n_ctx_tiles)
        yf = _ssd_forward(pf, pb, consts)
        if last:
            out = _mix_backward(pf, pb, yf, xa, ctx_in, modsel[l], w_out, l, consts, n_ctx_blocks,
                                final_norm_w)
        else:
            xa = _mix_backward(pf, pb, yf, xa, ctx_in, modsel[l], w_out, l, consts, n_ctx_blocks)
            ctx_in = None
    return out
```

```python
import functools

import numpy as np
import jax
import jax.numpy as jnp
from jax import lax
from jax.experimental import pallas as pl
from jax.experimental.pallas import tpu as pltpu

F32 = jnp.float32
BF16 = jnp.bfloat16

EPS = 1e-6
NEG_INF = -1e30
ROPE_BASE = 10000.0
GRID_W = 64

D_MODEL = 1024
CHUNK = 128
HALO = 8
ROW_TILE = 256
SSD_WIDTH = 1024
SSD_HEADS = 16
SSD_GROUPS = 2
SSD_STATE = 128
GROUP_W = SSD_WIDTH // SSD_GROUPS
BC_WIDTH = 2 * SSD_GROUPS * SSD_STATE
SC_WIDTH = 512
ATTN_WIDTH = 512
ATTN_HEADS = 8
ATTN_REP = 4
HEAD_DIM = 64
MIX_WIDTH = 2048
LANES = 128

W_X = 0
W_BC = 1024
W_VC = 1536
W_ZS = 2560
W_BZ = 3584
W_Q = 4608
W_ZA = 5120
W_KV = 5632
W_DT = 5888
PW = 6016
_W_GROUPS = ((1024, W_X, 1024), (2048, W_BC, 512), (2592, W_VC, 1024), (0, W_ZS, 1024),
             (3616, W_BZ, 1024), (4640, W_Q, 512), (5408, W_ZA, 512), (5152, W_KV, 256),
             (2560, W_DT, 32))
IN_COLS = 5920

F_XC = 0
F_ZS = 1024
F_GA = 2048
F_DT = 2560
FW = 2688
B_Q = 0
B_BC = 1024
B_YC = 1536
B_KV = 2048
BW = 2304
SCAN_ROWS = 2 * CHUNK

VMEM_LIMIT = 56 * 1024 * 1024


def _sigmoid(v):
    return 1.0 / (1.0 + jnp.exp(-v))


def _silu(v):
    return v * _sigmoid(v)


def _softplus(v):
    return jnp.maximum(v, 0.0) + jnp.log1p(jnp.exp(-jnp.abs(v)))


def _split2(v):
    hi = v.astype(BF16)
    lo = (v - hi.astype(F32)).astype(BF16)
    return hi, lo


def _split3(v):
    hi = v.astype(BF16)
    r = v - hi.astype(F32)
    mid = r.astype(BF16)
    lo = (r - mid.astype(F32)).astype(BF16)
    return hi, mid, lo


def _dot(a, b):
    return jnp.dot(a, b, preferred_element_type=F32)


def _dot_nt(a, b):
    return lax.dot_general(a, b, (((1,), (1,)), ((), ())), preferred_element_type=F32)


def _rmsnorm(x, w):
    ms = jnp.mean(x * x, axis=-1, keepdims=True)
    return x * lax.rsqrt(ms + EPS) * w


def _mod_kernel(c_ref, w_ref, b_ref, o_ref):
    sc = _silu(c_ref[...])
    o_ref[...] = _dot(sc.astype(BF16), w_ref[...].astype(BF16)) + b_ref[...]


def _modulation(cvec, w_mod, b_mod):
    depth = w_mod.shape[0]
    nblk = w_mod.shape[2] // D_MODEL
    return pl.pallas_call(
        _mod_kernel,
        grid=(depth, nblk),
        in_specs=[
            pl.BlockSpec((8, D_MODEL), lambda l, j: (0, 0)),
            pl.BlockSpec((None, D_MODEL, D_MODEL), lambda l, j: (l, 0, j)),
            pl.BlockSpec((None, 1, D_MODEL), lambda l, j: (l, 0, j)),
        ],
        out_specs=pl.BlockSpec((None, 8, D_MODEL), lambda l, j: (l, 0, j)),
        out_shape=jax.ShapeDtypeStruct((depth, 8, w_mod.shape[2]), F32),
        compiler_params=pltpu.CompilerParams(vmem_limit_bytes=VMEM_LIMIT),
        name="modulation",
    )(cvec, w_mod, b_mod.reshape(depth, 1, -1))


def _conv3(t, prev_row, next_row, w_ref):
    n = t.shape[0]
    rows = lax.broadcasted_iota(jnp.int32, t.shape, 0)
    tp = jnp.where(rows == 0, prev_row, pltpu.roll(t, 1, 0))
    tn = jnp.where(rows == n - 1, next_row, pltpu.roll(t, n - 1, 0))
    return w_ref[0:1, :] * tp + w_ref[1:2, :] * t + w_ref[2:3, :] * tn


def _rope(t, cos, sin_signed):
    lane = lax.broadcasted_iota(jnp.int32, t.shape, 1)
    first = (lane % 32) < 16
    swapped = jnp.where(first, pltpu.roll(t, LANES - 16, 1), pltpu.roll(t, 16, 1))
    return t * cos + swapped * sin_signed


def _inproj_kernel(split, n_ctx_tiles, n_tiles, *refs):
    if split:
        ctx_ref, refs = refs[0], refs[1:]
    (x_ref, xp_ref, xn_ref, mod_ref, nw_ref, wraw_ref, cwx_ref, cbx_ref, cwbc_ref,
     cbbc_ref, cwsc_ref, rope_ref, f_ref, b_ref, w_ref) = refs
    i = pl.program_id(1)

    @pl.when(jnp.logical_and(pl.program_id(0) == 0, i == 0))
    def _():
        for src, dst, width in _W_GROUPS:
            for c0 in range(0, width, 512):
                cw = min(512, width - c0)
                w_ref[:, dst + c0:dst + c0 + cw] = wraw_ref[src + c0:src + c0 + cw, :].T.astype(BF16)
        pad0 = W_DT + 2 * SSD_HEADS
        w_ref[:, pad0:PW] = jnp.zeros((D_MODEL, PW - pad0), BF16)

    has_prev = jnp.logical_and(i != 0, i != n_ctx_tiles).astype(F32)
    has_next = jnp.logical_and(i != n_ctx_tiles - 1, i != n_tiles - 1).astype(F32)

    def norm_mod(x):
        return _rmsnorm(x, nw_ref[...]) * (1.0 + mod_ref[1:2, :]) + mod_ref[0:1, :]

    x_tile = x_ref[...]
    if split:
        x_tile = jnp.where(i < n_ctx_tiles, ctx_ref[...], x_tile)
    h32 = norm_mod(x_tile)
    h = h32.astype(BF16)
    h_ext = jnp.concatenate([h32, norm_mod(xp_ref[...]), norm_mod(xn_ref[...])], axis=0).astype(BF16)
    prev_row, next_row = ROW_TILE + HALO - 1, ROW_TILE + HALO

    def proj(col, width):
        return _dot(h, w_ref[:, col:col + width])

    def proj_ext(col, width):
        pe = _dot(h_ext, w_ref[:, col:col + width])
        return pe[0:ROW_TILE], pe[prev_row:prev_row + 1], pe[next_row:next_row + 1]

    half = SSD_WIDTH // 2
    for k in range(2):
        p, pv, nx = proj_ext(W_X + k * half, half)
        y = _conv3(p, pv * has_prev, nx * has_next, cwx_ref.at[:, k * half:(k + 1) * half])
        f_ref[:, F_XC + k * half:F_XC + (k + 1) * half] = _silu(y + cbx_ref[:, k * half:(k + 1) * half])
    p, pv, nx = proj_ext(W_BC, BC_WIDTH)
    y = _conv3(p, pv * has_prev, nx * has_next, cwbc_ref) + cbbc_ref[...]
    b_ref[:, B_BC:B_BC + BC_WIDTH] = _silu(y).astype(BF16)

    w = SC_WIDTH
    vc, vcp, vcn = proj_ext(W_VC, 2 * w)
    conv = _conv3(vc[:, 0:w] * vc[:, w:2 * w], vcp[:, 0:w] * vcp[:, w:2 * w] * has_prev,
                  vcn[:, 0:w] * vcn[:, w:2 * w] * has_next, cwsc_ref)
    bz = proj(W_BZ, 2 * w)
    b_ref[:, B_YC:B_YC + w] = (bz[:, 0:w] * conv * _silu(bz[:, w:2 * w])).astype(BF16)

    for k in range(2):
        f_ref[:, F_ZS + k * half:F_ZS + (k + 1) * half] = _silu(proj(W_ZS + k * half, half))
    f_ref[:, F_GA:F_GA + ATTN_WIDTH] = _silu(proj(W_ZA, ATTN_WIDTH))
    f_ref[:, F_DT:F_DT + LANES] = proj(W_DT, LANES)

    cos, sin = rope_ref[:, 0:LANES], rope_ref[:, LANES:2 * LANES]
    left = lax.broadcasted_iota(jnp.int32, (ROW_TILE, LANES), 1) < HEAD_DIM
    q = proj(W_Q, ATTN_WIDTH)
    scale = HEAD_DIM ** -0.5
    for j in range(ATTN_WIDTH // LANES):
        qs = _rope(q[:, j * LANES:(j + 1) * LANES], cos, sin) * scale
        qr = pltpu.roll(qs, HEAD_DIM, 1)
        if j // 2 == 0:
            h0, h1 = jnp.where(left, qs, 0.0), jnp.where(left, qr, 0.0)
        else:
            h0, h1 = jnp.where(left, 0.0, qr), jnp.where(left, 0.0, qs)
        b_ref[:, B_Q + 2 * j * LANES:B_Q + (2 * j + 1) * LANES] = h0.astype(BF16)
        b_ref[:, B_Q + (2 * j + 1) * LANES:B_Q + (2 * j + 2) * LANES] = h1.astype(BF16)
    kv = proj(W_KV, 2 * LANES)
    b_ref[:, B_KV:B_KV + LANES] = _rope(kv[:, 0:LANES], cos, sin).astype(BF16)
    b_ref[:, B_KV + LANES:B_KV + 2 * LANES] = kv[:, LANES:2 * LANES].astype(BF16)


def _inproj(xa, ctx, modsel, norm_w, w_in, layer, consts, rope_tab, n_ctx_tiles):
    split = ctx is not None
    b, rows, d = xa.shape
    off = n_ctx_tiles if split else 0
    t = rows + off * ROW_TILE
    n_tiles = t // ROW_TILE
    per = ROW_TILE // HALO
    n_halo = rows // HALO

    def const(shape):
        return pl.BlockSpec(shape, lambda bi, i: (0,) * len(shape))

    in_specs = [
            pl.BlockSpec((None, ROW_TILE, d), lambda bi, i: (bi, jnp.maximum(i - off, 0), 0)),
            pl.BlockSpec((None, HALO, d),
                         lambda bi, i: (bi, jnp.maximum((i - off) * per - 1, 0), 0)),
            pl.BlockSpec((None, HALO, d),
                         lambda bi, i: (bi, jnp.clip((i - off + 1) * per, 0, n_halo - 1), 0)),
            pl.BlockSpec((None, None, 3, d),
                         lambda bi, i: (bi, jnp.where(i >= n_ctx_tiles, 1, 0), 0, 0)),
            const((1, d)),
            pl.BlockSpec((None, IN_COLS, d), lambda bi, i: (layer, 0, 0), pipeline_mode=pl.Buffered(1)),
            const((3, SSD_WIDTH)), const((1, SSD_WIDTH)), const((3, BC_WIDTH)), const((1, BC_WIDTH)),
            const((3, SC_WIDTH)),
            pl.BlockSpec((ROW_TILE, 2 * LANES), lambda bi, i: (i, 0)),
    ]
    args = [xa, xa, xa, modsel, norm_w.reshape(1, d), w_in, consts["cw_x"], consts["cb_x"],
            consts["cw_bc"], consts["cb_bc"], consts["cw_sc"], rope_tab]
    if split:
        assert ctx.shape[1] == ROW_TILE and n_ctx_tiles == 1
        in_specs.insert(0, pl.BlockSpec((None, ROW_TILE, d), lambda bi, i: (bi, 0, 0)))
        args.insert(0, ctx)
    return pl.pallas_call(
        functools.partial(_inproj_kernel, split, n_ctx_tiles, n_tiles),
        grid=(b, n_tiles),
        in_specs=in_specs,
        out_specs=[
            pl.BlockSpec((None, ROW_TILE, FW), lambda bi, i: (bi, i, 0)),
            pl.BlockSpec((None, ROW_TILE, BW), lambda bi, i: (bi, i, 0)),
        ],
        out_shape=[jax.ShapeDtypeStruct((b, t, FW), F32), jax.ShapeDtypeStruct((b, t, BW), BF16)],
        scratch_shapes=[pltpu.VMEM((d, PW), BF16)],
        compiler_params=pltpu.CompilerParams(
            dimension_semantics=("arbitrary", "arbitrary"), vmem_limit_bytes=VMEM_LIMIT),
        name="in_projection",
    )(*args)


def _scan_chunk(step, direction, n_ctx_chunks, n_chunks):
    if direction == 0:
        return step
    return jnp.where(step < n_ctx_chunks, n_ctx_chunks - 1 - step,
                     n_chunks - 1 - (step - n_ctx_chunks))


def _ssd_prepare(direction, xc, dt_raw, dtb_ref, alog_ref, e_ref):
    lane = lax.broadcasted_iota(jnp.int32, (1, LANES), 1)
    mine = jnp.logical_and(lane >= SSD_HEADS * direction, lane < SSD_HEADS * (direction + 1))
    a_row = jnp.where(mine, -jnp.exp(alog_ref[...]), 0.0)
    dt = _softplus(dt_raw + dtb_ref[...])
    da = dt * a_row

    r_i = lax.broadcasted_iota(jnp.int32, (CHUNK, CHUNK), 0)
    c_i = lax.broadcasted_iota(jnp.int32, (CHUNK, CHUNK), 1)
    tri = (c_i <= r_i) if direction == 0 else (c_i >= r_i)
    tri_b = jnp.where(tri, 1.0, 0.0).astype(BF16)
    u3 = _dot(tri_b, jnp.concatenate(_split3(da), axis=1))
    u = u3[:, 0:LANES] + u3[:, LANES:2 * LANES] + u3[:, 2 * LANES:3 * LANES]
    tot_row = CHUNK - 1 if direction == 0 else 0
    tot = u[tot_row:tot_row + 1, :]
    exp_u = jnp.exp(u)
    exp_d = jnp.exp(tot - u)

    stack = jnp.concatenate((dt.astype(BF16), exp_d.astype(BF16), exp_u.astype(BF16)), axis=0)
    ex = _dot(stack, e_ref[...])
    xdt = xc * ex[0:CHUNK]
    xdec = xdt * ex[CHUNK:2 * CHUNK]
    eu_e = ex[2 * CHUNK:3 * CHUNK]
    return dict(direction=direction, tri=tri, u=u, u_t=u.T, xdt=xdt, xdec=xdec, eu_e=eu_e,
                tot_row=tot_row)


def _ssd_output(prep, bc, state_ref):
    direction, tri, u, u_t = prep["direction"], prep["tri"], prep["u"], prep["u_t"]
    xdt, eu_e = prep["xdt"], prep["eu_e"]
    half = lax.broadcasted_iota(jnp.int32, (CHUNK, LANES), 1) < HEAD_DIM
    y_slabs = []
    for g in range(SSD_GROUPS):
        b_g = bc[:, g * SSD_STATE:(g + 1) * SSD_STATE]
        c_g = bc[:, (SSD_GROUPS + g) * SSD_STATE:(SSD_GROUPS + g + 1) * SSD_STATE]
        cb = _dot_nt(c_g, b_g)
        y_off = _dot(c_g, state_ref[g].astype(BF16)) * eu_e[:, g * GROUP_W:(g + 1) * GROUP_W]
        for j in range(GROUP_W // LANES):
            ms = []
            for hh in range(2):
                hl = SSD_HEADS * direction + g * (SSD_HEADS // SSD_GROUPS) + 2 * j + hh
                diff = u[:, hl:hl + 1] - u_t[hl:hl + 1, :]
                ms.append(cb * jnp.exp(jnp.where(tri, diff, -jnp.inf)))
            m2 = jnp.concatenate(ms, axis=1).astype(BF16)
            lo = g * GROUP_W + j * LANES
            slab = xdt[:, lo:lo + LANES]
            xbd = jnp.concatenate([jnp.where(half, slab, 0.0), jnp.where(half, 0.0, slab)],
                                  axis=0).astype(BF16)
            y_slabs.append(_dot(m2, xbd) + y_off[:, j * LANES:(j + 1) * LANES])
    return jnp.concatenate(y_slabs, axis=1)


def _ssd_update_state(prep, bc, state_ref):
    eu_e, xdec, tot_row = prep["eu_e"], prep["xdec"], prep["tot_row"]
    for g in range(SSD_GROUPS):
        b_g = bc[:, g * SSD_STATE:(g + 1) * SSD_STATE]
        decay = eu_e[tot_row:tot_row + 1, g * GROUP_W:(g + 1) * GROUP_W]
        state_ref[g] = state_ref[g] * decay + _dot(
            b_g.astype(F32).T.astype(BF16), xdec[:, g * GROUP_W:(g + 1) * GROUP_W].astype(BF16))


def _ssd_fwd_kernel(n_samples, xc_ref, bc_ref, dt_ref, dtb_ref, alog_ref, e_ref, o_ref, state_ref):
    @pl.when(pl.program_id(1) == 0)
    def _():
        state_ref[...] = jnp.zeros_like(state_ref)

    for k in range(SCAN_ROWS // CHUNK):
        rows = slice(k * CHUNK, (k + 1) * CHUNK)
        for smp in range(n_samples):
            prep = _ssd_prepare(0, xc_ref[smp, rows, :], dt_ref[smp, rows, :], dtb_ref, alog_ref, e_ref)
            bc = bc_ref[smp, rows, :]
            o_ref[smp, rows, :] = _ssd_output(prep, bc, state_ref.at[smp])
            _ssd_update_state(prep, bc, state_ref.at[smp])


def _ssd_forward(pf, pb, consts):
    b, t, _ = pf.shape
    ns = 2 if b % 2 == 0 else 1
    return pl.pallas_call(
        functools.partial(_ssd_fwd_kernel, ns),
        grid=(b // ns, t // SCAN_ROWS),
        in_specs=[
            pl.BlockSpec((ns, SCAN_ROWS, SSD_WIDTH), lambda bi, s: (bi, s, F_XC // SSD_WIDTH)),
            pl.BlockSpec((ns, SCAN_ROWS, BC_WIDTH), lambda bi, s: (bi, s, B_BC // BC_WIDTH)),
            pl.BlockSpec((ns, SCAN_ROWS, LANES), lambda bi, s: (bi, s, F_DT // LANES)),
            pl.BlockSpec((None, 1, LANES), lambda bi, s: (0, 0, 0)),
            pl.BlockSpec((None, 1, LANES), lambda bi, s: (0, 0, 0)),
            pl.BlockSpec((None, LANES, SSD_WIDTH), lambda bi, s: (0, 0, 0)),
        ],
        out_specs=pl.BlockSpec((ns, SCAN_ROWS, SSD_WIDTH), lambda bi, s: (bi, s, 0)),
        out_shape=jax.ShapeDtypeStruct((b, t, SSD_WIDTH), F32),
        scratch_shapes=[pltpu.VMEM((ns, SSD_GROUPS, SSD_STATE, GROUP_W), F32)],
        compiler_params=pltpu.CompilerParams(
            dimension_semantics=("parallel", "arbitrary"), vmem_limit_bytes=VMEM_LIMIT),
        name="ssd_forward",
    )(pf, pb, pf, consts["dt_bias"], consts["a_log"], consts["expand"])


def _attn_scores(q_ref, rows, k_all, bias):
    scores = []
    for j in range(ATTN_HEADS // 2):
        q2 = jnp.concatenate([q_ref[rows, 2 * j * LANES:(2 * j + 1) * LANES],
                              q_ref[rows, (2 * j + 1) * LANES:(2 * j + 2) * LANES]], axis=0)
        scores.append(_dot_nt(q2, k_all) + bias)
    return scores


def _attn_probs(scores, sink_ref):
    n_keys = scores[0].shape[1]
    probs, sink_terms = [], []
    for j, s in enumerate(scores):
        sk = jnp.concatenate([
            jnp.broadcast_to(sink_ref[2 * j:2 * j + 1, :], (CHUNK, LANES)),
            jnp.broadcast_to(sink_ref[2 * j + 1:2 * j + 2, :], (CHUNK, LANES))], axis=0)
        m = jnp.maximum(jnp.broadcast_to(jnp.max(s, axis=-1, keepdims=True), (2 * CHUNK, LANES)), sk)
        probs.append(jnp.concatenate(
            [jnp.exp(s[:, i * LANES:(i + 1) * LANES] - m).astype(BF16) for i in range(n_keys // LANES)],
            axis=1))
        sink_terms.append(jnp.exp(sk - m))
    return probs, sink_terms


def _attn_values(probs, sink_terms, v_all):
    left = lax.broadcasted_iota(jnp.int32, (CHUNK, LANES), 1) < HEAD_DIM
    v_ones = jnp.concatenate([v_all, jnp.ones((v_all.shape[0], LANES), BF16)], axis=1)
    outs = []
    for j in range(ATTN_HEADS // 2):
        pv = _dot(probs[j], v_ones)
        o = pv[:, 0:LANES] / (pv[:, LANES:2 * LANES] + sink_terms[j])
        o0, o1 = o[0:CHUNK], o[CHUNK:2 * CHUNK]
        if j // 2 == 0:
            outs.append(jnp.where(left, o0, pltpu.roll(o1, HEAD_DIM, 1)))
        else:
            outs.append(jnp.where(left, pltpu.roll(o0, HEAD_DIM, 1), o1))
    return outs


def _mix_bwd_kernel(final, split, n_samples, n_ctx_blocks, n_blocks, *refs):
    if split:
        ctx_ref, refs = refs[0], refs[1:]
    (xc_ref, bc_ref, dt_ref, zs_ref, yf_ref, yc_ref, ga_ref, q_ref, kv_ref, kvp_ref, kvn_ref,
     kvc_ref, x_ref, mod_ref, wraw_ref, dtb_ref, alog_ref, e_ref, dsk_ref, nw_ref, sink_ref,
     bias_ref) = refs[:22]
    if final:
        fw_ref, o_ref, state_ref, ycat_ref, w_ref = refs[22:]
    else:
        o_ref, state_ref, ycat_ref, w_ref = refs[22:]

    step = pl.program_id(1)

    @pl.when(jnp.logical_and(pl.program_id(0) == 0, step == 0))
    def _():
        for r0 in range(0, MIX_WIDTH, 512):
            w_ref[r0:r0 + 512, :] = wraw_ref[r0:r0 + 512, :].astype(BF16)

    blk = _scan_chunk(step, 1, n_ctx_blocks, n_blocks)
    per = SCAN_ROWS // CHUNK
    n_ctx_chunks, n_chunks = per * n_ctx_blocks, per * n_blocks

    @pl.when(step == 0)
    def _():
        state_ref[...] = jnp.zeros_like(state_ref)

    col = lax.broadcasted_iota(jnp.int32, (1, bias_ref.shape[1]), 1)
    for k in reversed(range(per)):
        rows = slice(k * CHUNK, (k + 1) * CHUNK)
        c = blk * per + k
        latent = c >= n_ctx_chunks
        no_prev = jnp.where(c > n_ctx_chunks, 0.0, NEG_INF)
        no_cur = jnp.where(latent, 0.0, NEG_INF)
        no_next = jnp.where(jnp.logical_and(latent, c < n_chunks - 1), 0.0, NEG_INF)
        edge = jnp.where(col < CHUNK, no_prev,
                         jnp.where(col < 2 * CHUNK, no_cur, jnp.where(col < 3 * CHUNK, no_next, 0.0)))
        bias = bias_ref[...] + edge
        for smp in range(n_samples):
            out_rows = slice(smp * SCAN_ROWS + k * CHUNK, smp * SCAN_ROWS + (k + 1) * CHUNK)
            xc = xc_ref[smp, rows, :]
            bc = bc_ref[smp, rows, :]
            prep = _ssd_prepare(1, xc, dt_ref[smp, rows, :], dtb_ref, alog_ref, e_ref)

            y = _ssd_output(prep, bc, state_ref.at[smp]) + yf_ref[smp, rows, :] + xc * dsk_ref[...]
            gated = y * zs_ref[smp, rows, :]
            for g in range(SSD_GROUPS):
                gg = gated[:, g * GROUP_W:(g + 1) * GROUP_W]
                ms = jnp.mean(gg * gg, axis=-1, keepdims=True)
                ycat_ref[out_rows, g * GROUP_W:(g + 1) * GROUP_W] = (
                    gg * lax.rsqrt(ms + EPS) * nw_ref[:, g * GROUP_W:(g + 1) * GROUP_W]).astype(BF16)
            _ssd_update_state(prep, bc, state_ref.at[smp])

            halves = ([kvp_ref.at[smp]] + [kv_ref.at[smp, i * CHUNK:(i + 1) * CHUNK, :] for i in range(per)]
                      + [kvn_ref.at[smp]])
            win = halves[k:k + 3] + [kvc_ref.at[smp]]
            k_all = jnp.concatenate([r[:, 0:LANES] for r in win], axis=0)
            v_all = jnp.concatenate([r[:, LANES:2 * LANES] for r in win], axis=0)
            scores = _attn_scores(q_ref.at[smp], rows, k_all, bias)
            probs, sink_terms = _attn_probs(scores, sink_ref)
            slabs = _attn_values(probs, sink_terms, v_all)
            ycat_ref[out_rows, SSD_WIDTH + SC_WIDTH:MIX_WIDTH] = (
                jnp.concatenate(slabs, axis=1) * ga_ref[smp, rows, :]).astype(BF16)

    yc_all = jnp.concatenate([yc_ref[smp] for smp in range(n_samples)], axis=0)
    acc = _dot(ycat_ref[:, 0:SSD_WIDTH], w_ref[0:SSD_WIDTH, :])
    acc = acc + _dot(yc_all, w_ref[SSD_WIDTH:SSD_WIDTH + SC_WIDTH, :])
    acc = acc + _dot(ycat_ref[:, SSD_WIDTH + SC_WIDTH:MIX_WIDTH], w_ref[SSD_WIDTH + SC_WIDTH:MIX_WIDTH, :])
    for smp in range(n_samples):
        x_in = x_ref[smp]
        if split:
            x_in = jnp.where(blk < n_ctx_blocks, ctx_ref[smp], x_in)
        xn = x_in + mod_ref[smp, 2:3, :] * acc[smp * SCAN_ROWS:(smp + 1) * SCAN_ROWS]
        if final:
            xn = _rmsnorm(xn, fw_ref[...])
        o_ref[smp] = xn


def _mix_backward(pf, pb, yf, xa, ctx, modsel, w_out, layer, consts, n_ctx_blocks, final_norm_w=None):
    split = ctx is not None
    b, t, _ = pf.shape
    d = xa.shape[-1]
    n_blocks = t // SCAN_ROWS
    per = SCAN_ROWS // CHUNK
    n_chunks = per * n_blocks
    final = final_norm_w is not None
    n_ctx = n_ctx_blocks * SCAN_ROWS
    x_off = n_ctx_blocks if split else 0

    def bidx(s):
        return _scan_chunk(s, 1, n_ctx_blocks, n_blocks)

    ns = 2 if b % 2 == 0 else 1

    def blk(width, offset):
        return pl.BlockSpec((ns, SCAN_ROWS, width), lambda bi, s: (bi, bidx(s), offset // width))

    def const(shape):
        return pl.BlockSpec(shape, lambda bi, s: (0,) * len(shape))

    kvw = 2 * LANES
    in_specs = [
        blk(SSD_WIDTH, F_XC), blk(BC_WIDTH, B_BC), blk(LANES, F_DT), blk(SSD_WIDTH, F_ZS),
        blk(SSD_WIDTH, 0),
        blk(SC_WIDTH, B_YC), blk(ATTN_WIDTH, F_GA), blk(2 * ATTN_WIDTH, B_Q),
        blk(kvw, B_KV),
        pl.BlockSpec((ns, CHUNK, kvw),
                     lambda bi, s: (bi, jnp.maximum(bidx(s) * per - 1, 0), B_KV // kvw)),
        pl.BlockSpec((ns, CHUNK, kvw),
                     lambda bi, s: (bi, jnp.minimum((bidx(s) + 1) * per, n_chunks - 1), B_KV // kvw)),
        pl.BlockSpec((ns, n_ctx, kvw), lambda bi, s: (bi, 0, B_KV // kvw)),
        pl.BlockSpec((ns, SCAN_ROWS, d),
                     lambda bi, s: (bi, jnp.maximum(bidx(s) - x_off, 0), 0)),
        pl.BlockSpec((ns, None, 3, d),
                     lambda bi, s: (bi, jnp.where(bidx(s) >= n_ctx_blocks, 1, 0), 0, 0)),
        pl.BlockSpec((None, MIX_WIDTH, d), lambda bi, s: (layer, 0, 0), pipeline_mode=pl.Buffered(1)),
        pl.BlockSpec((None, 1, LANES), lambda bi, s: (1, 0, 0)),
        pl.BlockSpec((None, 1, LANES), lambda bi, s: (1, 0, 0)),
        pl.BlockSpec((None, LANES, SSD_WIDTH), lambda bi, s: (1, 0, 0)),
        const((1, SSD_WIDTH)), const((1, SSD_WIDTH)), const((ATTN_HEADS, LANES)),
        const((2 * CHUNK, 3 * CHUNK + n_ctx)),
    ]
    args = [pf, pb, pf, pf, yf, pb, pf, pb, pb, pb, pb, pb, xa, modsel, w_out,
            consts["dt_bias"], consts["a_log"], consts["expand"], consts["d_skip"],
            consts["ssd_norm_w"], consts["sink"], consts["band_bias"]]
    if split:
        assert ctx.shape[1] == SCAN_ROWS and n_ctx_blocks == 1
        in_specs.insert(0, pl.BlockSpec((ns, SCAN_ROWS, d), lambda bi, s: (bi, 0, 0)))
        args.insert(0, ctx)
    if final:
        in_specs.append(const((1, d)))
        args.append(final_norm_w.reshape(1, d))
        out_spec = pl.BlockSpec(
            (ns, SCAN_ROWS, d),
            lambda bi, s: (bi, jnp.where(bidx(s) >= n_ctx_blocks, bidx(s), n_blocks - 1) - n_ctx_blocks, 0))
        out_shape = jax.ShapeDtypeStruct((b, t - n_ctx, d), F32)
    else:
        out_spec = blk(d, 0)
        out_shape = jax.ShapeDtypeStruct((b, t, d), F32)
    return pl.pallas_call(
        functools.partial(_mix_bwd_kernel, final, split, ns, n_ctx_blocks, n_blocks),
        grid=(b // ns, n_blocks),
        in_specs=in_specs,
        out_specs=out_spec,
        out_shape=out_shape,
        scratch_shapes=[pltpu.VMEM((ns, SSD_GROUPS, SSD_STATE, GROUP_W), F32),
                        pltpu.VMEM((ns * SCAN_ROWS, MIX_WIDTH), BF16),
                        pltpu.VMEM((MIX_WIDTH, d), BF16)],
        compiler_params=pltpu.CompilerParams(
            dimension_semantics=("arbitrary", "arbitrary"), vmem_limit_bytes=VMEM_LIMIT),
        name="mix_backward_final" if final else "mix_backward",
    )(*args)


def _rope_table(n_ctx, n_lat):
    n_rows = n_lat // GRID_W
    rows = np.repeat(np.arange(n_rows), GRID_W).astype(np.float32)
    cols = np.tile(np.arange(GRID_W), n_rows).astype(np.float32)
    axis_dim = HEAD_DIM // 2
    inv_freq = np.float32(ROPE_BASE) ** (-np.arange(0, axis_dim, 2, dtype=np.float32) / np.float32(axis_dim))
    ang = np.concatenate([rows[:, None] * inv_freq, cols[:, None] * inv_freq], axis=-1)
    cos, sin = np.cos(ang).astype(np.float32), np.sin(ang).astype(np.float32)
    q = HEAD_DIM // 4
    cos_h = np.concatenate([cos[:, 0:q], cos[:, 0:q], cos[:, q:2 * q], cos[:, q:2 * q]], axis=1)
    sin_h = np.concatenate([-sin[:, 0:q], sin[:, 0:q], -sin[:, q:2 * q], sin[:, q:2 * q]], axis=1)
    tab = np.concatenate([cos_h, cos_h, sin_h, sin_h], axis=1)
    ident = np.concatenate([np.ones((n_ctx, LANES), np.float32), np.zeros((n_ctx, LANES), np.float32)], axis=1)
    return jnp.asarray(np.concatenate([ident, tab], axis=0))


def _expand_matrix():
    e = np.zeros((2, LANES, SSD_WIDTH), np.float32)
    for d in range(2):
        for h in range(SSD_HEADS):
            e[d, d * SSD_HEADS + h, h * HEAD_DIM:(h + 1) * HEAD_DIM] = 1.0
    return jnp.asarray(e, BF16)


def _band_bias(n_ctx):
    row = np.arange(2 * CHUNK)[:, None] % CHUNK
    col = np.arange(3 * CHUNK + n_ctx)[None, :]
    ok = (col >= 3 * CHUNK) | ((col >= row) & (col <= row + 2 * CHUNK))
    return jnp.asarray(np.where(ok, 0.0, NEG_INF), F32)


def _dir_rows(v):
    out = jnp.zeros((2, 1, LANES), F32)
    for d in range(2):
        out = out.at[d, 0, d * SSD_HEADS:(d + 1) * SSD_HEADS].set(v[d])
    return out


def kernel(x, c, ctx, c_ctx, norm_w, w_mod, b_mod, w_in, ssd_conv_w, ssd_conv_b, ssd_dt_bias,
           ssd_a_log, ssd_d, ssd_norm_w, sc_conv_w, attn_sink, w_out, final_norm_w):
    b, n_lat, d = x.shape
    n_ctx = ctx.shape[1]
    depth = w_in.shape[0]
    assert d == D_MODEL and n_ctx % ROW_TILE == 0 and n_lat % ROW_TILE == 0 and b <= 7
    n_ctx_chunks = n_ctx // CHUNK
    n_ctx_tiles = n_ctx // ROW_TILE

    cvec = jnp.zeros((8, d), F32).at[0:b].set(c).at[b].set(c_ctx)
    mod = _modulation(cvec, w_mod, b_mod)
    mod = mod.reshape(depth, 8, 3, d)
    modsel = jnp.stack([jnp.broadcast_to(mod[:, b][:, None], (depth, b, 3, d)), mod[:, 0:b]], axis=2)

    xa, ctx_in = x, ctx
    assert n_ctx == ROW_TILE == SCAN_ROWS
    n_ctx_blocks = n_ctx // SCAN_ROWS
    rope_tab = _rope_table(n_ctx, n_lat)
    expand = _expand_matrix()
    band_bias = _band_bias(n_ctx)

    w_in_t = jnp.swapaxes(w_in, 1, 2)

    out = None
    for l in range(depth):
        last = l == depth - 1
        cw = ssd_conv_w[l]
        cbias = ssd_conv_b[l]
        consts = {
            "cw_x": cw[:, 0:SSD_WIDTH], "cb_x": cbias[0:SSD_WIDTH].reshape(1, -1),
            "cw_bc": cw[:, SSD_WIDTH:], "cb_bc": cbias[SSD_WIDTH:].reshape(1, -1),
            "cw_sc": sc_conv_w[l],
            "dt_bias": _dir_rows(ssd_dt_bias[l]), "a_log": _dir_rows(ssd_a_log[l]),
            "expand": expand,
            "d_skip": jnp.repeat(ssd_d[l], HEAD_DIM).reshape(1, -1),
            "ssd_norm_w": ssd_norm_w[l].reshape(1, -1),
            "sink": jnp.broadcast_to(attn_sink[l][:, None], (ATTN_HEADS, LANES)),
            "band_bias": band_bias,
        }
        pf, pb = _inproj(xa, ctx_in, modsel[l], norm_w[l], w_in_t, l, consts, rope_tab, n_ctx_tiles)
        yf = _ssd_forward(pf, pb, consts)
        if last:
            out = _mix_backward(pf, pb, yf, xa, ctx_in, modsel[l], w_out, l, consts, n_ctx_blocks,
                                final_norm_w)
        else:
            xa = _mix_backward(pf, pb, yf, xa, ctx_in, modsel[l], w_out, l, consts, n_ctx_blocks)
            ctx_in = None
    return out
```
